```python
import jax
import jax.numpy as jnp
from jax import lax
import numpy as np

D_MODEL = 1024
BATCH = 1
SEQ = 16384
DEPTH = 4

GRID_W = 64
CTX_LEN = 256
N_MIXERS = 3
RMS_EPS = 1e-6
ROPE_THETA = 10000.0

NA_HEADS = 16
NA_HEAD_DIM = D_MODEL // NA_HEADS
NA_WIN_ROWS = 8
NA_WIN_COLS = 16

SWA_Q_HEADS = 16
SWA_KV_HEADS = 4
SWA_HEAD_DIM = D_MODEL // SWA_Q_HEADS
SWA_GROUP = SWA_Q_HEADS // SWA_KV_HEADS
SWA_WINDOW = 128
SWA_BLOCK = 128

MLA_HEADS = 16
MLA_Q_RANK = D_MODEL // 2
MLA_KV_RANK = D_MODEL // 4
MLA_NOPE_DIM = 64
MLA_ROPE_DIM = 32
MLA_V_DIM = 64
MLA_QK_DIM = MLA_NOPE_DIM + MLA_ROPE_DIM
MLA_Q_BLOCK = 128

N_EXPERTS = 16
EXPERT_FF = 2 * D_MODEL
EC_CAPACITY_FACTOR = 2

kernel_name = 'hybrid_na_swa_mla_ecmoe_dit'


def rmsnorm(x, g):
    xf = x.astype(jnp.float32)
    y = xf * lax.rsqrt(jnp.mean(xf * xf, axis=-1, keepdims=True) + RMS_EPS)
    return (y * g.astype(jnp.float32)).astype(x.dtype)


def modulate(h, shift, scale):
    return h * (1 + scale) + shift


def axial_rope_angles(n_tokens, rot_dim):
    quarter = rot_dim // 4
    inv_freq = ROPE_THETA ** (-jnp.arange(quarter, dtype=jnp.float32) / quarter)
    t = jnp.arange(n_tokens)
    row = (t // GRID_W).astype(jnp.float32)
    col = (t % GRID_W).astype(jnp.float32)
    return row[:, None] * inv_freq, col[:, None] * inv_freq


def _rotate(x, ang):
    m = ang.shape[-1]
    bshape = (ang.shape[0],) + (1,) * (x.ndim - 3) + (m,)
    cos = jnp.cos(ang).reshape(bshape).astype(x.dtype)
    sin = jnp.sin(ang).reshape(bshape).astype(x.dtype)
    x1, x2 = x[..., :m], x[..., m:]
    return jnp.concatenate([x1 * cos - x2 * sin, x1 * sin + x2 * cos], axis=-1)


def apply_axial_rope(x, ang_row, ang_col):
    half = x.shape[-1] // 2
    return jnp.concatenate([_rotate(x[..., :half], ang_row), _rotate(x[..., half:], ang_col)], axis=-1)


def dense_attend(q, k, v, scale, sink=None):
    s = jnp.einsum('bqkgd,bskd->bkgqs', q, k).astype(jnp.float32) * scale
    if sink is None:
        p = jax.nn.softmax(s, axis=-1)
    else:
        s_sink = jnp.broadcast_to(sink.astype(jnp.float32)[None, :, :, None, None], s.shape[:-1] + (1,))
        p = jax.nn.softmax(jnp.concatenate([s_sink, s], axis=-1), axis=-1)[..., 1:]
    return jnp.einsum('bkgqs,bskd->bqkgd', p.astype(v.dtype), v)


def neighborhood_attention(hc, hl, w_qkv, w_o, rpb, need_ctx):
    B, L, _ = hl.shape
    rows = L // GRID_W
    kr, kc = min(NA_WIN_ROWS, rows), NA_WIN_COLS
    scale = NA_HEAD_DIM ** -0.5
    q, k, v = jnp.moveaxis((hl @ w_qkv).reshape(B, L, 3, NA_HEADS, NA_HEAD_DIM), 2, 0)
    qc, k_ctx, v_ctx = jnp.moveaxis((hc @ w_qkv).reshape(B, -1, 3, NA_HEADS, NA_HEAD_DIM), 2, 0)
    qg = q.reshape(B, rows, GRID_W, NA_HEADS, NA_HEAD_DIM)
    kg = k.reshape(B, rows, GRID_W, NA_HEADS, NA_HEAD_DIM)
    vg = v.reshape(B, rows, GRID_W, NA_HEADS, NA_HEAD_DIM)
    col = jnp.arange(GRID_W)
    col_idx = jnp.clip(col - kc // 2, 0, GRID_W - kc)[:, None] + jnp.arange(kc)
    dc = col_idx - col[:, None] + (NA_WIN_COLS - 1)
    row_start = jnp.clip(jnp.arange(rows) - kr // 2, 0, rows - kr)

    def one_row(r):
        rs = row_start[r]
        k_nb = lax.dynamic_slice_in_dim(kg, rs, kr, axis=1)[:, :, col_idx]
        v_nb = lax.dynamic_slice_in_dim(vg, rs, kr, axis=1)[:, :, col_idx]
        q_r = lax.dynamic_index_in_dim(qg, r, axis=1, keepdims=False)
        dr = rs + jnp.arange(kr) - r + (NA_WIN_ROWS - 1)
        bias = rpb[:, dr[None, :, None], dc[:, None, :]]
        s_nb = jnp.einsum('bqhd,bpqkhd->bhqpk', q_r, k_nb).astype(jnp.float32) * scale + bias.astype(jnp.float32)
        s_ctx = jnp.einsum('bqhd,bchd->bhqc', q_r, k_ctx).astype(jnp.float32) * scale
        p = jax.nn.softmax(jnp.concatenate([s_nb.reshape(B, NA_HEADS, GRID_W, kr * kc), s_ctx], axis=-1), axis=-1)
        p = p.astype(v.dtype)
        p_nb = p[..., :kr * kc].reshape(B, NA_HEADS, GRID_W, kr, kc)
        p_ctx = p[..., kr * kc:]
        return jnp.einsum('bhqpk,bpqkhd->bqhd', p_nb, v_nb) + jnp.einsum('bhqc,bchd->bqhd', p_ctx, v_ctx)

    o = lax.map(one_row, jnp.arange(rows))
    yl = jnp.moveaxis(o, 0, 1).reshape(B, L, -1) @ w_o
    yc = None
    if need_ctx:
        oc = dense_attend(qc[:, :, :, None], k_ctx, v_ctx, scale)
        yc = oc.reshape(B, hc.shape[1], -1) @ w_o
    return yc, yl


def sliding_window_gqa(hc, hl, w_qkv, w_o, sink, need_ctx):
    B, L, _ = hl.shape
    nq, nkv = SWA_Q_HEADS * SWA_HEAD_DIM, SWA_KV_HEADS * SWA_HEAD_DIM
    scale = SWA_HEAD_DIM ** -0.5
    sink_logits = sink.reshape(SWA_KV_HEADS, SWA_GROUP)

    def split(h):
        n = h.shape[1]
        qkv = h @ w_qkv
        q = qkv[..., :nq].reshape(B, n, SWA_KV_HEADS, SWA_GROUP, SWA_HEAD_DIM)
        k = qkv[..., nq:nq + nkv].reshape(B, n, SWA_KV_HEADS, SWA_HEAD_DIM)
        v = qkv[..., nq + nkv:].reshape(B, n, SWA_KV_HEADS, SWA_HEAD_DIM)
        return q, k, v

    q_l, k_l, v_l = split(hl)
    q_c, k_c, v_c = split(hc)
    ang_row, ang_col = axial_rope_angles(L, SWA_HEAD_DIM)
    q_l = apply_axial_rope(q_l, ang_row, ang_col)
    k_l = apply_axial_rope(k_l, ang_row, ang_col)
    pad = ((0, 0), (SWA_WINDOW, SWA_WINDOW), (0, 0), (0, 0))
    kp, vp = jnp.pad(k_l, pad), jnp.pad(v_l, pad)
    band = SWA_BLOCK + 2 * SWA_WINDOW
    n_blocks = L // SWA_BLOCK

    def one_block(n):
        start = n * SWA_BLOCK
        qb = lax.dynamic_slice_in_dim(q_l, start, SWA_BLOCK, axis=1)
        kb = lax.dynamic_slice_in_dim(kp, start, band, axis=1)
        vb = lax.dynamic_slice_in_dim(vp, start, band, axis=1)
        qpos = start + jnp.arange(SWA_BLOCK)
        kpos = start - SWA_WINDOW + jnp.arange(band)
        valid = (jnp.abs(kpos[None, :] - qpos[:, None]) <= SWA_WINDOW) & (kpos >= 0)[None, :] & (kpos < L)[None, :]
        s_band = jnp.einsum('bqkgd,bskd->bkgqs', qb, kb).astype(jnp.float32) * scale
        s_band = jnp.where(valid, s_band, -jnp.inf)
        s_ctx = jnp.einsum('bqkgd,bskd->bkgqs', qb, k_c).astype(jnp.float32) * scale
        s_sink = jnp.broadcast_to(sink_logits.astype(jnp.float32)[None, :, :, None, None], s_ctx.shape[:-1] + (1,))
        p = jax.nn.softmax(jnp.concatenate([s_sink, s_band, s_ctx], axis=-1), axis=-1).astype(v_l.dtype)
        p_band, p_ctx = p[..., 1:1 + band], p[..., 1 + band:]
        return jnp.einsum('bkgqs,bskd->bqkgd', p_band, vb) + jnp.einsum('bkgqs,bskd->bqkgd', p_ctx, v_c)

    o = lax.map(one_block, jnp.arange(n_blocks))
    yl = jnp.moveaxis(o, 0, 1).reshape(B, L, nq) @ w_o
    yc = None
    if need_ctx:
        oc = dense_attend(q_c, k_c, v_c, scale, sink_logits)
        yc = oc.reshape(B, hc.shape[1], nq) @ w_o
    return yc, yl


def latent_attention(hc, hl, w_dq, q_norm, w_uq, w_dkv, kv_norm, w_ukv, w_o, need_ctx):
    B, L, _ = hl.shape
    scale = MLA_QK_DIM ** -0.5

    def project(h):
        n = h.shape[1]
        q = (rmsnorm(h @ w_dq, q_norm) @ w_uq).reshape(B, n, MLA_HEADS, MLA_QK_DIM)
        ckv = h @ w_dkv
        kv = (rmsnorm(ckv[..., :MLA_KV_RANK], kv_norm) @ w_ukv).reshape(B, n, MLA_HEADS, MLA_NOPE_DIM + MLA_V_DIM)
        k_rope = ckv[..., MLA_KV_RANK:][:, :, None, :]
        return q, kv[..., :MLA_NOPE_DIM], k_rope, kv[..., MLA_NOPE_DIM:]

    def join_key(k_nope, k_rope):
        return jnp.concatenate([k_nope, jnp.broadcast_to(k_rope, k_nope.shape[:-1] + (MLA_ROPE_DIM,))], axis=-1)

    ang_row, ang_col = axial_rope_angles(L, MLA_ROPE_DIM)
    q_l, kn_l, kr_l, v_l = project(hl)
    q_l = jnp.concatenate([q_l[..., :MLA_NOPE_DIM], apply_axial_rope(q_l[..., MLA_NOPE_DIM:], ang_row, ang_col)], axis=-1)
    k_l = join_key(kn_l, apply_axial_rope(kr_l, ang_row, ang_col))
    q_c, kn_c, kr_c, v_c = project(hc)
    k_c = join_key(kn_c, kr_c)
    k_all = jnp.concatenate([k_l, k_c], axis=1)
    v_all = jnp.concatenate([v_l, v_c], axis=1)
    n_blocks = L // MLA_Q_BLOCK
    q_blocks = jnp.moveaxis(q_l.reshape(B, n_blocks, MLA_Q_BLOCK, MLA_HEADS, 1, MLA_QK_DIM), 1, 0)
    o = lax.map(lambda qb: dense_attend(qb, k_all, v_all, scale), q_blocks)
    yl = jnp.moveaxis(o, 0, 1).reshape(B, L, MLA_HEADS * MLA_V_DIM) @ w_o
    yc = None
    if need_ctx:
        oc = dense_attend(q_c[:, :, :, None], k_c, v_c, scale)
        yc = oc.reshape(B, hc.shape[1], MLA_HEADS * MLA_V_DIM) @ w_o
    return yc, yl


def expert_choice_ffn(h, router, w_gate, w_up, w_down):
    B, N, D = h.shape
    cap = EC_CAPACITY_FACTOR * N // N_EXPERTS
    aff = jax.nn.softmax(h.astype(jnp.float32) @ router.astype(jnp.float32), axis=-1)
    gate, idx = lax.top_k(jnp.swapaxes(aff, 1, 2), cap)
    b_idx = jnp.arange(B)[:, None]
    x_in = h[b_idx, idx.reshape(B, -1)].reshape(B, N_EXPERTS, cap, D)
    a = jnp.einsum('becd,edf->becf', x_in, w_gate)
    u = jnp.einsum('becd,edf->becf', x_in, w_up)
    y = jnp.einsum('becf,efd->becd', jax.nn.silu(a) * u, w_down) * gate[..., None].astype(h.dtype)
    return jnp.zeros_like(h).at[b_idx[:, :, None], idx].add(y)


def setup_inputs(seed: int = 0) -> dict:
    key = jax.random.key(seed)
    ks = iter(jax.random.split(key, 40))

    def nrm(shape, std):
        return std * jax.random.normal(next(ks), shape, jnp.float32)

    D = D_MODEL
    n_na = len(range(0, DEPTH, N_MIXERS))
    n_swa = len(range(1, DEPTH, N_MIXERS))
    n_mla = len(range(2, DEPTH, N_MIXERS))
    swa_cols = (SWA_Q_HEADS + 2 * SWA_KV_HEADS) * SWA_HEAD_DIM
    return {
        'x': nrm((BATCH, SEQ, D), 1.0),
        'c': nrm((BATCH, D), 1.0),
        'ctx': nrm((BATCH, CTX_LEN, D), 1.0),
        'c_ctx': nrm((D,), 1.0),
        'ada_w': nrm((DEPTH, D, 6 * D), 0.5 * D ** -0.5),
        'ada_b': nrm((DEPTH, 6 * D), 0.02),
        'norm_mix': 1.0 + nrm((DEPTH, D), 0.1),
        'norm_ffn': 1.0 + nrm((DEPTH, D), 0.1),
        'na_w_qkv': nrm((n_na, D, 3 * NA_HEADS * NA_HEAD_DIM), D ** -0.5),
        'na_w_o': nrm((n_na, NA_HEADS * NA_HEAD_DIM, D), (NA_HEADS * NA_HEAD_DIM) ** -0.5),
        'na_rpb': nrm((n_na, NA_HEADS, 2 * NA_WIN_ROWS - 1, 2 * NA_WIN_COLS - 1), 0.1),
        'swa_w_qkv': nrm((n_swa, D, swa_cols), D ** -0.5),
        'swa_w_o': nrm((n_swa, SWA_Q_HEADS * SWA_HEAD_DIM, D), (SWA_Q_HEADS * SWA_HEAD_DIM) ** -0.5),
        'swa_sink': nrm((n_swa, SWA_Q_HEADS), 0.5),
        'mla_w_dq': nrm((n_mla, D, MLA_Q_RANK), D ** -0.5),
        'mla_q_norm': 1.0 + nrm((n_mla, MLA_Q_RANK), 0.1),
        'mla_w_uq': nrm((n_mla, MLA_Q_RANK, MLA_HEADS * MLA_QK_DIM), MLA_Q_RANK ** -0.5),
        'mla_w_dkv': nrm((n_mla, D, MLA_KV_RANK + MLA_ROPE_DIM), D ** -0.5),
        'mla_kv_norm': 1.0 + nrm((n_mla, MLA_KV_RANK), 0.1),
        'mla_w_ukv': nrm((n_mla, MLA_KV_RANK, MLA_HEADS * (MLA_NOPE_DIM + MLA_V_DIM)), MLA_KV_RANK ** -0.5),
        'mla_w_o': nrm((n_mla, MLA_HEADS * MLA_V_DIM, D), (MLA_HEADS * MLA_V_DIM) ** -0.5),
        'moe_router': nrm((DEPTH, D, N_EXPERTS), D ** -0.5),
        'moe_w_gate': nrm((DEPTH, N_EXPERTS, D, EXPERT_FF), D ** -0.5),
        'moe_w_up': nrm((DEPTH, N_EXPERTS, D, EXPERT_FF), D ** -0.5),
        'moe_w_down': nrm((DEPTH, N_EXPERTS, EXPERT_FF, D), EXPERT_FF ** -0.5),
        'final_norm': 1.0 + nrm((D,), 0.1),
    }


def reference(x, c, ctx, c_ctx, ada_w, ada_b, norm_mix, norm_ffn, na_w_qkv, na_w_o, na_rpb, swa_w_qkv, swa_w_o, swa_sink, mla_w_dq, mla_q_norm, mla_w_uq, mla_w_dkv, mla_kv_norm, mla_w_ukv, mla_w_o, moe_router, moe_w_gate, moe_w_up, moe_w_down, final_norm):
    B = x.shape[0]
    xl, xc = x, ctx
    for i in range(DEPTH):
        need_ctx = i < DEPTH - 1
        ml = (jax.nn.silu(c) @ ada_w[i] + ada_b[i]).reshape(B, 6, 1, D_MODEL)
        mc = (jax.nn.silu(c_ctx) @ ada_w[i] + ada_b[i]).reshape(6, D_MODEL)
        hl = modulate(rmsnorm(xl, norm_mix[i]), ml[:, 0], ml[:, 1])
        hc = modulate(rmsnorm(xc, norm_mix[i]), mc[0], mc[1])
        kind, slot = i % N_MIXERS, i // N_MIXERS
        if kind == 0:
            yc, yl = neighborhood_attention(hc, hl, na_w_qkv[slot], na_w_o[slot], na_rpb[slot], need_ctx)
        elif kind == 1:
            yc, yl = sliding_window_gqa(hc, hl, swa_w_qkv[slot], swa_w_o[slot], swa_sink[slot], need_ctx)
        else:
            yc, yl = latent_attention(hc, hl, mla_w_dq[slot], mla_q_norm[slot], mla_w_uq[slot], mla_w_dkv[slot], mla_kv_norm[slot], mla_w_ukv[slot], mla_w_o[slot], need_ctx)
        xl = xl + ml[:, 2] * yl
        hl = modulate(rmsnorm(xl, norm_ffn[i]), ml[:, 3], ml[:, 4])
        xl = xl + ml[:, 5] * expert_choice_ffn(hl, moe_router[i], moe_w_gate[i], moe_w_up[i], moe_w_down[i])
        if need_ctx:
            xc = xc + mc[2] * yc
            hc = modulate(rmsnorm(xc, norm_ffn[i]), mc[3], mc[4])
            xc = xc + mc[5] * expert_choice_ffn(hc, moe_router[i], moe_w_gate[i], moe_w_up[i], moe_w_down[i])
    return rmsnorm(xl, final_norm)
```

```python
import functools
import math

import jax
import jax.numpy as jnp
import numpy as np
from jax import lax
from jax.experimental import pallas as pl
from jax.experimental.pallas import tpu as pltpu

F32 = jnp.float32
BF16 = jnp.bfloat16
HIGHEST = lax.Precision.HIGHEST

GRID_W = 64
N_MIXERS = 3
RMS_EPS = 1e-6
ROPE_THETA = 10000.0
NA_HEADS = 16
NA_WIN_ROWS = 8
NA_WIN_COLS = 16
SWA_Q_HEADS = 16
SWA_KV_HEADS = 4
SWA_WINDOW = 128
MLA_HEADS = 16
MLA_NOPE_DIM = 64
MLA_ROPE_DIM = 32
MLA_V_DIM = 64
N_EXPERTS = 16
EC_CAPACITY_FACTOR = 2

TB = 256
CHUNK = 128
LANES = 128
NEG = -1e30
FFN_ROWS = 208
GATHER_WIN = TB + 16
VMEM_LIMIT = 56 * 1024 * 1024


def _cparams(*sem):
    return pltpu.CompilerParams(dimension_semantics=sem, vmem_limit_bytes=VMEM_LIMIT)


def _ada_kernel(cs_ref, w_ref, b_ref, o_ref):
    x = cs_ref[...]
    x = x * jax.nn.sigmoid(x)
    y = jnp.dot(x, w_ref[0], precision=HIGHEST, preferred_element_type=F32)
    o_ref[0] = y + b_ref[0]


def _ada(cs, ada_w, ada_b):
    depth, d, n = ada_w.shape
    nb = n // d
    return pl.pallas_call(
        _ada_kernel,
        grid=(depth, nb),
        in_specs=[
            pl.BlockSpec((8, d), lambda i, k: (0, 0)),
            pl.BlockSpec((1, d, d), lambda i, k: (i, 0, k)),
            pl.BlockSpec((1, 1, d), lambda i, k: (i, 0, k)),
        ],
        out_specs=pl.BlockSpec((1, 8, d), lambda i, k: (i, 0, k)),
        out_shape=jax.ShapeDtypeStruct((depth, 8, n), F32),
        compiler_params=_cparams("arbitrary", "arbitrary"),
        name="ada",
    )(cs, ada_w, ada_b.reshape(depth, 1, n))


def _normmod(x, g, shift, scale):
    y = x * lax.rsqrt(jnp.mean(x * x, axis=-1, keepdims=True) + RMS_EPS)
    return (y * g) * (1.0 + scale) + shift


def _proj_kernel(*refs, normmod, rms_segs, rope_cols, colscale, resid, splits):
    it = iter(refs)
    x_ref = next(it)
    if normmod:
        g_ref, sh_ref, sc_ref = next(it), next(it), next(it)
    w_ref = next(it)
    if rms_segs:
        g2_ref = next(it)
    if rope_cols:
        w2_ref, cos_ref, sin_ref = next(it), next(it), next(it)
    if colscale:
        cs_ref = next(it)
    if resid:
        res_ref, gate_ref = next(it), next(it)
    out_refs = list(it)

    x = x_ref[...]
    if normmod:
        x = _normmod(x.astype(F32), g_ref[...], sh_ref[0, 0], sc_ref[0, 0])
    xb = x.astype(BF16)
    y = jnp.dot(xb, w_ref[...], preferred_element_type=F32)
    if rms_segs:
        parts = []
        prev = 0
        for (a, b) in rms_segs:
            if a > prev:
                parts.append(y[:, prev:a])
            seg = y[:, a:b]
            seg = seg * lax.rsqrt(jnp.mean(seg * seg, axis=-1, keepdims=True) + RMS_EPS)
            parts.append(seg * g2_ref[:, a:b])
            prev = b
        if prev < y.shape[1]:
            parts.append(y[:, prev:])
        y = jnp.concatenate(parts, axis=1)
    if rope_cols:
        y2 = jnp.dot(xb, w2_ref[...], preferred_element_type=F32)
        reps = rope_cols // LANES
        cos = jnp.tile(cos_ref[...], (1, reps))
        sin = jnp.tile(sin_ref[...], (1, reps))
        yr = y[:, :rope_cols] * cos + y2 * sin
        y = yr if rope_cols == y.shape[1] else jnp.concatenate([yr, y[:, rope_cols:]], axis=1)
    if colscale:
        y = y * cs_ref[...]
    if resid:
        y = res_ref[...] + gate_ref[0, 0] * y
    for o_ref, (a, b) in zip(out_refs, splits):
        o_ref[...] = y[:, a:b].astype(o_ref.dtype)


def _proj(x, w, *, ntl, mods=None, norm_g=None, mod_idx=None, rms_segs=None, rms_g=None,
          w2=None, cos=None, sin=None, colscale=None, resid=None, gate_idx=None,
          splits=None, out_dtypes=None, name="proj"):
    t, kdim = x.shape
    n = w.shape[1]
    nt = t // TB
    splits = splits or [(0, n)]
    out_dtypes = out_dtypes or [BF16] * len(splits)

    def modspec(k):
        dm = mods.shape[-1]
        return pl.BlockSpec((1, 1, 1, dm), lambda i: (k, i // ntl, 0, 0))

    args, specs = [x], [pl.BlockSpec((TB, kdim), lambda i: (i, 0))]
    if norm_g is not None:
        args += [norm_g.reshape(1, kdim), mods, mods]
        specs += [pl.BlockSpec((1, kdim), lambda i: (0, 0)), modspec(mod_idx[0]), modspec(mod_idx[1])]
    args.append(w)
    specs.append(pl.BlockSpec((kdim, n), lambda i: (0, 0)))
    if rms_segs:
        args.append(rms_g.reshape(1, n))
        specs.append(pl.BlockSpec((1, n), lambda i: (0, 0)))
    rope_cols = 0
    if w2 is not None:
        rope_cols = w2.shape[1]
        args += [w2, cos, sin]
        specs += [pl.BlockSpec((kdim, rope_cols), lambda i: (0, 0)),
                  pl.BlockSpec((TB, LANES), lambda i: (i, 0)),
                  pl.BlockSpec((TB, LANES), lambda i: (i, 0))]
    if colscale is not None:
        args.append(colscale.reshape(1, n))
        specs.append(pl.BlockSpec((1, n), lambda i: (0, 0)))
    if resid is not None:
        args += [resid, mods]
        specs += [pl.BlockSpec((TB, n), lambda i: (i, 0)), modspec(gate_idx)]
    out_shape = [jax.ShapeDtypeStruct((t, b - a), dt) for (a, b), dt in zip(splits, out_dtypes)]
    out_specs = [pl.BlockSpec((TB, b - a), lambda i: (i, 0)) for (a, b) in splits]
    kern = functools.partial(
        _proj_kernel, normmod=norm_g is not None, rms_segs=rms_segs, rope_cols=rope_cols,
        colscale=colscale is not None, resid=resid is not None, splits=splits)
    outs = pl.pallas_call(
        kern, grid=(nt,), in_specs=specs, out_specs=out_specs, out_shape=out_shape,
        compiler_params=_cparams("arbitrary"), name=name,
    )(*args)
    return outs


def _attn_kernel(*refs, kind, dk, dv, ntl):
    if kind == "na":
        q_ref, k_ref, v_ref, bias_ref, o_ref = refs
    elif kind == "swa":
        sink_ref, q_ref, k_ref, v_ref, o_ref = refs
    else:
        q_ref, k_ref, v_ref, o_ref = refs
    hp = pl.program_id(0)
    i = pl.program_id(1)
    is_lat = i < ntl
    ts = jnp.clip(i - 1, 0, ntl - 3)
    q_shift = jnp.where(is_lat, 0, 1 << 24)
    ctx_pen = jnp.where(is_lat, 0.0, NEG)

    outs = []
    for hh in range(2):
        q = q_ref[:, hh * dk:(hh + 1) * dk]

        def tile_step(kt, carry, extra):
            m, l, acc = carry
            start = pl.multiple_of(kt * TB, TB)
            k = k_ref[pl.ds(start, TB), hh * dk:(hh + 1) * dk]
            v = v_ref[pl.ds(start, TB), hh * dv:(hh + 1) * dv]
            s = lax.dot_general(q, k, (((1,), (1,)), ((), ())), preferred_element_type=F32)
            if extra is not None:
                s = extra(s, kt)
            m_new = jnp.maximum(m, jnp.max(s, axis=-1, keepdims=True))
            alpha = jnp.exp(m - m_new)
            p = jnp.exp(s - m_new)
            l = alpha * l + jnp.sum(p, axis=-1, keepdims=True)
            acc = alpha * acc + jnp.dot(p.astype(BF16), v, preferred_element_type=F32)
            return m_new, l, acc

        carry = (jnp.full((TB, 1), NEG, F32), jnp.zeros((TB, 1), F32), jnp.zeros((TB, dv), F32))
        if kind == "mla":
            hi = jnp.where(is_lat, ntl, 0)
            carry = lax.fori_loop(0, hi, lambda kt, c: tile_step(kt, c, None), carry)
        else:
            for j in range(3):
                if kind == "swa":
                    def extra(s, kt):
                        qpos = i * TB + q_shift + lax.broadcasted_iota(jnp.int32, (TB, TB), 0)
                        kpos = kt * TB + lax.broadcasted_iota(jnp.int32, (TB, TB), 1)
                        return jnp.where(jnp.abs(kpos - qpos) <= SWA_WINDOW, s, NEG)
                else:
                    def extra(s, kt, j=j):
                        return s + bias_ref[0, hh, :, j * TB:(j + 1) * TB] + ctx_pen
                carry = tile_step(ts + j, carry, extra)
        m, l, acc = tile_step(ntl, carry, None)
        if kind == "swa":
            sink = sink_ref[2 * hp + hh]
            m_f = jnp.maximum(m, sink)
            a = jnp.exp(m - m_f)
            l = l * a + jnp.exp(sink - m_f)
            acc = acc * a
        outs.append(acc / l)
    o_ref[...] = jnp.concatenate(outs, axis=1).astype(o_ref.dtype)


def _attn(kind, q, k, v, *, dk, dv, ntl, bias=None, sink=None):
    t = q.shape[0]
    nt = t // TB
    nh = q.shape[1] // dk
    kern = functools.partial(_attn_kernel, kind=kind, dk=dk, dv=dv, ntl=ntl)
    in_specs = [
        pl.BlockSpec((TB, 2 * dk), lambda hp, i, *_: (i, hp)),
        pl.BlockSpec((t, 2 * dk), lambda hp, i, *_: (0, hp)),
        pl.BlockSpec((t, 2 * dv), lambda hp, i, *_: (0, hp)),
    ]
    args = [q, k, v]
    nsp = 0
    if kind == "na":
        def bias_map(hp, i):
            pat = jnp.where(i == 0, 0, jnp.where(i >= ntl - 1, 2, 1))
            return (pat, hp, 0, 0)
        in_specs.append(pl.BlockSpec((1, 2, TB, 3 * TB), bias_map))
        args.append(bias)
    if kind == "swa":
        nsp = 1
        args = [sink] + args
    grid_spec = pltpu.PrefetchScalarGridSpec(
        num_scalar_prefetch=nsp, grid=(nh // 2, nt), in_specs=in_specs,
        out_specs=pl.BlockSpec((TB, 2 * dv), lambda hp, i, *_: (i, hp)))
    return pl.pallas_call(
        kern, grid_spec=grid_spec,
        out_shape=jax.ShapeDtypeStruct((t, nh * dv), BF16),
        compiler_params=_cparams("arbitrary", "arbitrary"), name="attn_" + kind,
    )(*args)


def _router_kernel(x_ref, g_ref, sh_ref, sc_ref, rt_ref, h_ref, aff_ref):
    h = _normmod(x_ref[...], g_ref[...], sh_ref[0, 0], sc_ref[0, 0])
    h_ref[...] = h.astype(BF16)
    lg = lax.dot_general(rt_ref[...], h, (((1,), (1,)), ((), ())),
                         precision=HIGHEST, preferred_element_type=F32)
    lg = lg - jnp.max(lg, axis=0, keepdims=True)
    e = jnp.exp(lg)
    aff_ref[...] = e / jnp.sum(e, axis=0, keepdims=True)


def _router(xa, g, mods, router_t, ntl):
    t, d = xa.shape
    ne = router_t.shape[0]

    def modspec(k):
        return pl.BlockSpec((1, 1, 1, d), lambda i: (k, i // ntl, 0, 0))

    return pl.pallas_call(
        _router_kernel, grid=(t // TB,),
        in_specs=[pl.BlockSpec((TB, d), lambda i: (i, 0)),
                  pl.BlockSpec((1, d), lambda i: (0, 0)),
                  modspec(3), modspec(4),
                  pl.BlockSpec((ne, d), lambda i: (0, 0))],
        out_specs=[pl.BlockSpec((TB, d), lambda i: (i, 0)),
                   pl.BlockSpec((ne, TB), lambda i: (0, i))],
        out_shape=[jax.ShapeDtypeStruct((t, d), BF16), jax.ShapeDtypeStruct((ne, t), F32)],
        compiler_params=_cparams("arbitrary"), name="router",
    )(xa, g.reshape(1, d), mods, mods, router_t)


def _topk_kernel(aff_ref, pos_ref, gate_ref, off_ref, *, cap, base):
    x = aff_ref[...]
    ne, nc, _ = x.shape
    bits = lax.bitcast_convert_type(x, jnp.int32)

    def count(mask):
        c = jnp.sum(jnp.where(mask, 1.0, 0.0), axis=1, keepdims=True)
        return jnp.sum(c, axis=2, keepdims=True)

    thr = jnp.zeros((ne, 1, 1), jnp.int32)
    for b in range(30, -1, -1):
        cand = thr | (1 << b)
        thr = jnp.where(count(bits >= cand) >= cap, cand, thr)

    ia = lax.broadcasted_iota(jnp.int32, (LANES, LANES), 0)
    ib = lax.broadcasted_iota(jnp.int32, (LANES, LANES), 1)
    upper = jnp.where(ia <= ib, 1.0, 0.0)
    ones = jnp.ones((LANES, LANES), F32)
    ca = lax.broadcasted_iota(jnp.int32, (nc, nc), 0)
    cb = lax.broadcasted_iota(jnp.int32, (nc, nc), 1)
    lower = jnp.where(cb < ca, 1.0, 0.0)

    def prefix(mask):
        m2 = jnp.where(mask, 1.0, 0.0).reshape(ne * nc, LANES)
        within = jnp.dot(m2, upper, precision=HIGHEST, preferred_element_type=F32)
        tot = jnp.dot(m2, ones, precision=HIGHEST, preferred_element_type=F32).reshape(ne, nc, LANES)
        offs = [jnp.dot(lower, tot[e], precision=HIGHEST, preferred_element_type=F32)[None]
                for e in range(ne)]
        off = jnp.concatenate(offs, axis=0)
        return within.reshape(ne, nc, LANES) + off, off

    gt = bits > thr
    eq = bits == thr
    need = cap - count(gt)
    eq_rank, _ = prefix(eq)
    sel = gt | (eq & (eq_rank <= need))
    sel_rank, off = prefix(sel)
    pos_ref[...] = jnp.where(sel, sel_rank - 1.0 + base, -1.0)
    gate_ref[...] = jnp.where(sel, x, 0.0)
    off_ref[...] = off.astype(jnp.int32) + base


def _topk(aff3, cap, base):
    ne, nc, _ = aff3.shape
    kern = functools.partial(_topk_kernel, cap=cap, base=base)
    spec = pl.BlockSpec((ne, nc, LANES), lambda i: (0, 0, 0))
    return pl.pallas_call(
        kern, grid=(1,), in_specs=[spec], out_specs=[spec, spec, spec],
        out_shape=[jax.ShapeDtypeStruct(aff3.shape, F32), jax.ShapeDtypeStruct(aff3.shape, F32),
                   jax.ShapeDtypeStruct(aff3.shape, jnp.int32)],
        compiler_params=_cparams("arbitrary"), name="topk",
    )(aff3)


def _ffn_kernel(offs_ref, pos_ref, gate_ref, h_ref, wg_ref, wu_ref, wd_ref, y_ref, xs_ref, gs_ref,
                *, nt, nslots):
    e = pl.program_id(0)
    j = pl.program_id(1)

    @pl.when(j == 0)
    def _():
        xs_ref[...] = jnp.zeros_like(xs_ref)
        gs_ref[...] = jnp.zeros_like(gs_ref)

    @pl.when(j < nt)
    def _():
        off = offs_ref[e, j]
        w0 = pl.multiple_of((off // 8) * 8, 8)
        prow = pos_ref[0]
        grow = gate_ref[0]
        ids = (w0 + lax.broadcasted_iota(jnp.int32, (GATHER_WIN, 1), 0)).astype(F32)
        hit = prow == ids
        onehot = jnp.where(hit, 1.0, 0.0).astype(BF16)
        xs_ref[pl.ds(w0, GATHER_WIN), :] += jnp.dot(onehot, h_ref[...], preferred_element_type=F32)
        gsel = jnp.sum(jnp.where(hit, grow, 0.0), axis=1, keepdims=True)
        gs_ref[pl.ds(w0, GATHER_WIN), :] += jnp.broadcast_to(gsel, (GATHER_WIN, LANES))

    @pl.when(j >= nt)
    def _():
        b = j - nt
        r0 = pl.multiple_of(b * FFN_ROWS, 8)

        @pl.when(r0 < nslots)
        def _():
            x = xs_ref[pl.ds(r0, FFN_ROWS), :].astype(BF16)
            a = jnp.dot(x, wg_ref[0], preferred_element_type=F32)
            u = jnp.dot(x, wu_ref[0], preferred_element_type=F32)
            hmid = (a * jax.nn.sigmoid(a) * u).astype(BF16)
            y = jnp.dot(hmid, wd_ref[0], preferred_element_type=F32)
            y_ref[0] = (y * gs_ref[pl.ds(r0, FFN_ROWS), 0:1]).astype(y_ref.dtype)

        @pl.when(r0 >= nslots)
        def _():
            y_ref[...] = jnp.zeros_like(y_ref)


def _ffn(offs, pos, gate, h, wg, wu, wd, *, nslots, nblk):
    t, d = h.shape
    nt = t // TB
    ne, _, f = wg.shape
    xs_rows = nblk * FFN_ROWS + GATHER_WIN
    kern = functools.partial(_ffn_kernel, nt=nt, nslots=nslots)
    grid_spec = pltpu.PrefetchScalarGridSpec(
        num_scalar_prefetch=1, grid=(ne, nt + nblk),
        in_specs=[
            pl.BlockSpec((1, 1, TB), lambda e, j, o: (e, 0, jnp.minimum(j, nt - 1))),
            pl.BlockSpec((1, 1, TB), lambda e, j, o: (e, 0, jnp.minimum(j, nt - 1))),
            pl.BlockSpec((TB, d), lambda e, j, o: (jnp.minimum(j, nt - 1), 0)),
            pl.BlockSpec((1, d, f), lambda e, j, o: (e, 0, 0)),
            pl.BlockSpec((1, d, f), lambda e, j, o: (e, 0, 0)),
            pl.BlockSpec((1, f, d), lambda e, j, o: (e, 0, 0)),
        ],
        out_specs=pl.BlockSpec((1, FFN_ROWS, d), lambda e, j, o: (e, jnp.maximum(j - nt, 0), 0)),
        scratch_shapes=[pltpu.VMEM((xs_rows, d), F32), pltpu.VMEM((xs_rows, LANES), F32)])
    return pl.pallas_call(
        kern, grid_spec=grid_spec,
        out_shape=jax.ShapeDtypeStruct((ne, nblk * FFN_ROWS, d), BF16),
        compiler_params=_cparams("arbitrary", "arbitrary"), name="ffn",
    )(offs, pos.reshape(ne, 1, t), gate.reshape(ne, 1, t), h, wg, wu, wd)


def _combine_kernel(*refs, ne, srows):
    offs_ref, cnt_ref = refs[0], refs[1]
    posc_ref, xa_ref, gate_ref = refs[2], refs[3], refs[4]
    win_refs = refs[5:5 + ne]
    win2_refs = refs[5 + ne:5 + 2 * ne]
    o_ref = refs[5 + 2 * ne]
    acc_ref = refs[6 + 2 * ne]
    j = pl.program_id(0)
    acc_ref[...] = jnp.zeros_like(acc_ref)
    lane = lax.broadcasted_iota(jnp.int32, (1, TB), 1)
    for e in range(ne):
        off = offs_ref[e, j]
        cnt = cnt_ref[e, j]
        w0 = (off // 16) * 16
        pcol = posc_ref[:, e:e + 1]
        onehot = jnp.where(pcol == (w0 + lane).astype(F32), 1.0, 0.0).astype(BF16)
        acc_ref[...] += jnp.dot(onehot, win_refs[e][...], preferred_element_type=F32)

        @pl.when(off - w0 + cnt > TB)
        def _(e=e, w0=w0, pcol=pcol):
            w1 = jnp.minimum(w0 + TB, srows - TB)
            ids = w1 + lane
            hit = (pcol == ids.astype(F32)) & (ids >= w0 + TB)
            oh = jnp.where(hit, 1.0, 0.0).astype(BF16)
            acc_ref[...] += jnp.dot(oh, win2_refs[e][...], preferred_element_type=F32)

    o_ref[...] = xa_ref[...] + gate_ref[0, 0] * acc_ref[...]


def _combine(offs, cnts, pos_cols, xa, mods, ys, ntl):
    t, d = xa.shape
    nt = t // TB
    ne, srows, _ = ys.shape

    assert srows % 16 == 0
    ys2 = ys.reshape(ne * srows, d)

    def win_spec(e):
        def imap(j, offs, cnts):
            return ((e * srows // 16 + offs[e, j] // 16) * 16, 0)
        return pl.BlockSpec((pl.Element(TB), pl.Element(d)), imap)

    def win2_spec(e):
        def imap(j, offs, cnts):
            off = offs[e, j]
            w0 = (off // 16) * 16
            need = off - w0 + cnts[e, j] > TB
            w1 = jnp.where(need, jnp.minimum(w0 + TB, srows - TB), 0)
            return ((e * srows // 16 + w1 // 16) * 16, 0)
        return pl.BlockSpec((pl.Element(TB), pl.Element(d)), imap)

    in_specs = [
        pl.BlockSpec((TB, ne), lambda j, o, c: (j, 0)),
        pl.BlockSpec((TB, d), lambda j, o, c: (j, 0)),
        pl.BlockSpec((1, 1, 1, d), lambda j, o, c: (5, j // ntl, 0, 0)),
    ] + [win_spec(e) for e in range(ne)] + [win2_spec(e) for e in range(ne)]
    grid_spec = pltpu.PrefetchScalarGridSpec(
        num_scalar_prefetch=2, grid=(nt,), in_specs=in_specs,
        out_specs=pl.BlockSpec((TB, d), lambda j, o, c: (j, 0)),
        scratch_shapes=[pltpu.VMEM((TB, d), F32)])
    kern = functools.partial(_combine_kernel, ne=ne, srows=srows)
    return pl.pallas_call(
        kern, grid_spec=grid_spec, out_shape=jax.ShapeDtypeStruct((t, d), F32),
        compiler_params=_cparams("arbitrary"), name="combine",
    )(offs, cnts, pos_cols, xa, mods, *([ys2] * (2 * ne)))


def _final_kernel(x_ref, g_ref, o_ref):
    x = x_ref[...]
    o_ref[...] = x * lax.rsqrt(jnp.mean(x * x, axis=-1, keepdims=True) + RMS_EPS) * g_ref[...]


def _final_norm(xa, g, seq):
    d = xa.shape[1]
    return pl.pallas_call(
        _final_kernel, grid=(seq // TB,),
        in_specs=[pl.BlockSpec((TB, d), lambda i: (i, 0)), pl.BlockSpec((1, d), lambda i: (0, 0))],
        out_specs=pl.BlockSpec((TB, d), lambda i: (i, 0)),
        out_shape=jax.ShapeDtypeStruct((seq, d), F32),
        compiler_params=_cparams("arbitrary"), name="final_norm",
    )(xa, g.reshape(1, d))


def _rope_tables(seq, t, rot_dim, lane_layout):
    quarter = rot_dim // 4
    inv_freq = ROPE_THETA ** (-jnp.arange(quarter, dtype=F32) / quarter)
    tok = jnp.arange(seq)
    row = (tok // GRID_W).astype(F32)[:, None] * inv_freq
    col = (tok % GRID_W).astype(F32)[:, None] * inv_freq
    cos_blk = jnp.concatenate([jnp.cos(row), jnp.cos(row), jnp.cos(col), jnp.cos(col)], axis=1)
    sin_blk = jnp.concatenate([jnp.sin(row), jnp.sin(row), jnp.sin(col), jnp.sin(col)], axis=1)
    cos = jnp.ones((seq, LANES), F32)
    sin = jnp.zeros((seq, LANES), F32)
    for lo, hi in lane_layout:
        cos = cos.at[:, lo:hi].set(cos_blk)
        sin = sin.at[:, lo:hi].set(sin_blk)
    cos = jnp.concatenate([cos, jnp.ones((t - seq, LANES), F32)], axis=0)
    sin = jnp.concatenate([sin, jnp.zeros((t - seq, LANES), F32)], axis=0)
    return cos, sin


def _rope_partner_perm(rot_dim):
    q = rot_dim // 4
    src = np.concatenate([np.arange(q, 2 * q), np.arange(0, q), np.arange(3 * q, 4 * q), np.arange(2 * q, 3 * q)])
    sign = np.concatenate([-np.ones(q), np.ones(q), -np.ones(q), np.ones(q)]).astype(np.float32)
    return src, sign


def _na_bias(rpb, rows, ntl):
    rpt = TB // GRID_W
    out = []
    for i in (0, 1, ntl - 1):
        ts = min(max(i - 1, 0), ntl - 3)
        tq = np.arange(TB)
        tk = np.arange(3 * TB)
        r = i * rpt + tq // GRID_W
        c = tq % GRID_W
        r2 = ts * rpt + tk // GRID_W
        c2 = tk % GRID_W
        kr = min(NA_WIN_ROWS, rows)
        rs = np.clip(r - kr // 2, 0, rows - kr)
        cs = np.clip(c - NA_WIN_COLS // 2, 0, GRID_W - NA_WIN_COLS)
        ok = ((r2[None, :] >= rs[:, None]) & (r2[None, :] < rs[:, None] + kr)
              & (c2[None, :] >= cs[:, None]) & (c2[None, :] < cs[:, None] + NA_WIN_COLS))
        dr = np.clip(r2[None, :] - r[:, None] + NA_WIN_ROWS - 1, 0, 2 * NA_WIN_ROWS - 2)
        dc = np.clip(c2[None, :] - c[:, None] + NA_WIN_COLS - 1, 0, 2 * NA_WIN_COLS - 2)
        b = rpb[:, dr, dc]
        out.append(jnp.where(ok[None], b, NEG))
    return jnp.stack(out, axis=0).astype(F32)


def _mixer_na(xa, g, mods, w_qkv, w_o, rpb, ntl, seq):
    d = xa.shape[1]
    hd = d // NA_HEADS
    n = NA_HEADS * hd
    scale = hd ** -0.5
    colscale = jnp.concatenate([jnp.full((n,), scale, F32), jnp.ones((2 * n,), F32)])
    q, k, v = _proj(xa, w_qkv.astype(BF16), ntl=ntl, mods=mods, norm_g=g, mod_idx=(0, 1),
                    colscale=colscale, splits=[(0, n), (n, 2 * n), (2 * n, 3 * n)], name="na_qkv")
    bias = _na_bias(rpb, seq // GRID_W, ntl)
    o = _attn("na", q, k, v, dk=hd, dv=hd, ntl=ntl, bias=bias)
    (xa,) = _proj(o, w_o.astype(BF16), ntl=ntl, mods=mods, resid=xa, gate_idx=2,
                  out_dtypes=[F32], name="na_out")
    return xa


def _mixer_swa(xa, g, mods, w_qkv, w_o, sink, ntl, seq):
    t, d = xa.shape
    hd = d // SWA_Q_HEADS
    nq, nkv = SWA_Q_HEADS * hd, SWA_KV_HEADS * hd
    group = SWA_Q_HEADS // SWA_KV_HEADS
    scale = hd ** -0.5
    wq = w_qkv[:, :nq]
    wk = jnp.repeat(w_qkv[:, nq:nq + nkv].reshape(d, SWA_KV_HEADS, hd), group, axis=1).reshape(d, nq)
    wv = jnp.repeat(w_qkv[:, nq + nkv:].reshape(d, SWA_KV_HEADS, hd), group, axis=1).reshape(d, nq)
    w = jnp.concatenate([wq, wk, wv], axis=1)
    src, sign = _rope_partner_perm(hd)
    nblk = 2 * nq // hd
    src_full = (np.arange(nblk)[:, None] * hd + src[None, :]).reshape(-1)
    sign_full = np.tile(sign, nblk)
    w2 = w[:, src_full] * sign_full[None, :]
    cos, sin = _rope_tables(seq, t, hd, [(0, hd), (hd, 2 * hd)])
    colscale = jnp.concatenate([jnp.full((nq,), scale, F32), jnp.ones((2 * nq,), F32)])
    q, k, v = _proj(xa, w.astype(BF16), ntl=ntl, mods=mods, norm_g=g, mod_idx=(0, 1),
                    w2=w2.astype(BF16), cos=cos, sin=sin, colscale=colscale,
                    splits=[(0, nq), (nq, 2 * nq), (2 * nq, 3 * nq)], name="swa_qkv")
    o = _attn("swa", q, k, v, dk=hd, dv=hd, ntl=ntl, sink=sink.astype(F32))
    (xa,) = _proj(o, w_o.astype(BF16), ntl=ntl, mods=mods, resid=xa, gate_idx=2,
                  out_dtypes=[F32], name="swa_out")
    return xa


def _mixer_mla(xa, g, mods, w_dq, q_norm, w_uq, w_dkv, kv_norm, w_ukv, w_o, ntl, seq):
    t, d = xa.shape
    nh = MLA_HEADS
    qr = w_dq.shape[1]
    kvr = kv_norm.shape[0]
    nope, rope, dv = MLA_NOPE_DIM, MLA_ROPE_DIM, MLA_V_DIM
    qk = nope + rope
    pad = LANES - qk
    scale = qk ** -0.5
    kin = kvr + LANES
    w_dn = jnp.concatenate([w_dq, w_dkv, jnp.zeros((d, LANES - rope), F32)], axis=1)
    g_dn = jnp.concatenate([q_norm, kv_norm, jnp.ones((LANES,), F32)])
    cqn, kvin = _proj(xa, w_dn.astype(BF16), ntl=ntl, mods=mods, norm_g=g, mod_idx=(0, 1),
                      rms_segs=[(0, qr), (qr, qr + kvr)], rms_g=g_dn,
                      splits=[(0, qr), (qr, qr + kin)], name="mla_down")
    src, sign = _rope_partner_perm(rope)
    src_head = np.concatenate([np.arange(nope), nope + src, np.arange(qk, LANES)])
    sign_head = np.concatenate([np.zeros(nope, np.float32), sign, np.zeros(pad, np.float32)])
    src_full = (np.arange(nh)[:, None] * LANES + src_head[None, :]).reshape(-1)
    sign_full = np.tile(sign_head, nh)
    cos, sin = _rope_tables(seq, t, rope, [(nope, qk)])
    wq = jnp.pad(w_uq.reshape(qr, nh, qk), ((0, 0), (0, 0), (0, pad))).reshape(qr, nh * LANES)
    wq2 = wq[:, src_full] * sign_full[None, :]
    (q,) = _proj(cqn, wq.astype(BF16), ntl=ntl, w2=wq2.astype(BF16), cos=cos, sin=sin,
                 colscale=jnp.full((nh * LANES,), scale, F32), name="mla_q")
    w_ukv3 = w_ukv.reshape(kvr, nh, nope + dv)
    wk_top = jnp.pad(w_ukv3[:, :, :nope], ((0, 0), (0, 0), (0, LANES - nope)))
    eye = jnp.eye(rope, dtype=F32)
    wk_rope = jnp.pad(eye, ((0, LANES - rope), (nope, pad)))
    wk_bot = jnp.broadcast_to(wk_rope[:, None, :], (LANES, nh, LANES))
    wk = jnp.concatenate([wk_top, wk_bot], axis=0).reshape(kin, nh * LANES)
    wk2 = wk[:, src_full] * sign_full[None, :]
    wv = jnp.concatenate([w_ukv3[:, :, nope:].reshape(kvr, nh * dv), jnp.zeros((LANES, nh * dv), F32)], axis=0)
    (k,) = _proj(kvin, wk.astype(BF16), ntl=ntl, w2=wk2.astype(BF16), cos=cos, sin=sin, name="mla_k")
    (v,) = _proj(kvin, wv.astype(BF16), ntl=ntl, name="mla_v")
    o = _attn("mla", q, k, v, dk=LANES, dv=dv, ntl=ntl)
    (xa,) = _proj(o, w_o.astype(BF16), ntl=ntl, mods=mods, resid=xa, gate_idx=2,
                  out_dtypes=[F32], name="mla_out")
    return xa


def _moe(xa, g, mods, router, w_gate, w_up, w_down, ntl, seq, with_ctx):
    t, d = xa.shape
    ne = router.shape[1]
    nctx = t - seq
    h, aff = _router(xa, g, mods, router.T, ntl)
    cap_l = EC_CAPACITY_FACTOR * seq // ne
    pos_l, gate_l, off_l = _topk(aff[:, :seq].reshape(ne, seq // CHUNK, CHUNK), cap_l, 0)
    per_tile = TB // CHUNK
    pos = [pos_l.reshape(ne, seq)]
    gate = [gate_l.reshape(ne, seq)]
    offs = [off_l[:, ::per_tile, 0]]
    if with_ctx:
        cap_c = EC_CAPACITY_FACTOR * nctx // ne
        cpad = 8 * CHUNK
        aff_c = jnp.concatenate([aff[:, seq:], jnp.full((ne, cpad - nctx), -1.0, F32)], axis=1)
        pos_c, gate_c, off_c = _topk(aff_c.reshape(ne, 8, CHUNK), cap_c, cap_l)
        pos.append(pos_c.reshape(ne, cpad)[:, :nctx])
        gate.append(gate_c.reshape(ne, cpad)[:, :nctx])
        offs.append(off_c[:, 0:nctx // CHUNK:per_tile, 0])
        nslots = cap_l + cap_c
    else:
        pos.append(jnp.full((ne, nctx), -1.0, F32))
        gate.append(jnp.zeros((ne, nctx), F32))
        offs.append(jnp.full((ne, nctx // TB), cap_l, jnp.int32))
        nslots = cap_l
    pos = jnp.concatenate(pos, axis=1)
    gate = jnp.concatenate(gate, axis=1)
    offs = jnp.concatenate(offs + [jnp.full((ne, 1), nslots, jnp.int32)], axis=1)
    cnts = offs[:, 1:] - offs[:, :-1]
    nblk = -(-(nslots + TB) // FFN_ROWS)
    ys = _ffn(offs, pos, gate, h, w_gate.astype(BF16), w_up.astype(BF16), w_down.astype(BF16),
              nslots=nslots, nblk=nblk)
    return _combine(offs, cnts, pos.T, xa, mods, ys, ntl)


def kernel(x, c, ctx, c_ctx, ada_w, ada_b, norm_mix, norm_ffn, na_w_qkv, na_w_o, na_rpb, swa_w_qkv, swa_w_o, swa_sink, mla_w_dq, mla_q_norm, mla_w_uq, mla_w_dkv, mla_kv_norm, mla_w_ukv, mla_w_o, moe_router, moe_w_gate, moe_w_up, moe_w_down, final_norm):
    assert x.shape[0] == 1 and c.shape[0] == 1 and ctx.shape[0] == 1
    seq, d = x.shape[1], x.shape[2]
    nctx = ctx.shape[1]
    assert seq % TB == 0 and nctx == TB and seq // TB >= 4
    depth = ada_w.shape[0]
    ntl = seq // TB
    xa = jnp.concatenate([x[0], ctx[0]], axis=0)
    cs = jnp.concatenate([c, c_ctx[None, :], jnp.zeros((6, d), F32)], axis=0)
    ada = _ada(cs, ada_w, ada_b)
    mods_all = jnp.transpose(ada[:, :2].reshape(depth, 2, 6, d), (0, 2, 1, 3))[:, :, :, None, :]
    for i in range(depth):
        mods = mods_all[i]
        kind, slot = i % N_MIXERS, i // N_MIXERS
        if kind == 0:
            xa = _mixer_na(xa, norm_mix[i], mods, na_w_qkv[slot], na_w_o[slot], na_rpb[slot], ntl, seq)
        elif kind == 1:
            xa = _mixer_swa(xa, norm_mix[i], mods, swa_w_qkv[slot], swa_w_o[slot], swa_sink[slot], ntl, seq)
        else:
            xa = _mixer_mla(xa, norm_mix[i], mods, mla_w_dq[slot], mla_q_norm[slot], mla_w_uq[slot],
                            mla_w_dkv[slot], mla_kv_norm[slot], mla_w_ukv[slot], mla_w_o[slot], ntl, seq)
        xa = _moe(xa, norm_ffn[i], mods, moe_router[i], moe_w_gate[i], moe_w_up[i], moe_w_down[i],
                  ntl, seq, with_ctx=i < depth - 1)
    return _final_norm(xa, final_norm, seq)[None]
```

```python
import functools
import math

import jax
import jax.numpy as jnp
import numpy as np
from jax import lax
from jax.experimental import pallas as pl
from jax.experimental.pallas import tpu as pltpu

F32 = jnp.float32
BF16 = jnp.bfloat16
HIGHEST = lax.Precision.HIGHEST

GRID_W = 64
N_MIXERS = 3
RMS_EPS = 1e-6
ROPE_THETA = 10000.0
NA_HEADS = 16
NA_WIN_ROWS = 8
NA_WIN_COLS = 16
SWA_Q_HEADS = 16
SWA_KV_HEADS = 4
SWA_WINDOW = 128
MLA_HEADS = 16
MLA_NOPE_DIM = 64
MLA_ROPE_DIM = 32
MLA_V_DIM = 64
N_EXPERTS = 16
EC_CAPACITY_FACTOR = 2

TB = 256
CHUNK = 128
LANES = 128
NEG = -1e30
LOG2E = math.log2(math.e)
FFN_ROWS = 416
FFN_FCHUNK = 512
GATHER_WIN = TB + 16
VMEM_LIMIT = 56 * 1024 * 1024


def _cparams(*sem):
    return pltpu.CompilerParams(dimension_semantics=sem, vmem_limit_bytes=VMEM_LIMIT)


def _ada_kernel(cs_ref, w_ref, b_ref, o_ref):
    x = cs_ref[...]
    x = x * jax.nn.sigmoid(x)
    y = jnp.dot(x, w_ref[0], precision=HIGHEST, preferred_element_type=F32)
    o_ref[0] = y + b_ref[0]


def _ada(cs, ada_w, ada_b):
    depth, d, n = ada_w.shape
    nb = n // d
    return pl.pallas_call(
        _ada_kernel,
        grid=(depth, nb),
        in_specs=[
            pl.BlockSpec((8, d), lambda i, k: (0, 0)),
            pl.BlockSpec((1, d, d), lambda i, k: (i, 0, k)),
            pl.BlockSpec((1, 1, d), lambda i, k: (i, 0, k)),
        ],
        out_specs=pl.BlockSpec((1, 8, d), lambda i, k: (i, 0, k)),
        out_shape=jax.ShapeDtypeStruct((depth, 8, n), F32),
        compiler_params=_cparams("arbitrary", "arbitrary"),
        name="ada",
    )(cs, ada_w, ada_b.reshape(depth, 1, n))


def _normmod(x, g, shift, scale):
    y = x * lax.rsqrt(jnp.mean(x * x, axis=-1, keepdims=True) + RMS_EPS)
    return (y * g) * (1.0 + scale) + shift


def _proj_kernel(*refs, normmod, rms_segs, rope_cols, colscale, resid, splits):
    it = iter(refs)
    x_ref = next(it)
    if normmod:
        g_ref, sh_ref, sc_ref = next(it), next(it), next(it)
    w_ref = next(it)
    if rms_segs:
        g2_ref = next(it)
    if rope_cols:
        w2_ref, cos_ref, sin_ref = next(it), next(it), next(it)
    if colscale:
        cs_ref = next(it)
    if resid:
        res_ref, gate_ref = next(it), next(it)
    out_refs = list(it)

    x = x_ref[...]
    if normmod:
        x = _normmod(x.astype(F32), g_ref[...], sh_ref[0, 0], sc_ref[0, 0])
    xb = x.astype(BF16)
    y = jnp.dot(xb, w_ref[...], preferred_element_type=F32)
    if rms_segs:
        parts = []
        prev = 0
        for (a, b) in rms_segs:
            if a > prev:
                parts.append(y[:, prev:a])
            seg = y[:, a:b]
            seg = seg * lax.rsqrt(jnp.mean(seg * seg, axis=-1, keepdims=True) + RMS_EPS)
            parts.append(seg * g2_ref[:, a:b])
            prev = b
        if prev < y.shape[1]:
            parts.append(y[:, prev:])
        y = jnp.concatenate(parts, axis=1)
    if rope_cols:
        y2 = jnp.dot(xb, w2_ref[...], preferred_element_type=F32)
        reps = rope_cols // LANES
        cos = jnp.tile(cos_ref[...], (1, reps))
        sin = jnp.tile(sin_ref[...], (1, reps))
        yr = y[:, :rope_cols] * cos + y2 * sin
        y = yr if rope_cols == y.shape[1] else jnp.concatenate([yr, y[:, rope_cols:]], axis=1)
    if colscale:
        y = y * cs_ref[...]
    if resid:
        y = res_ref[...] + gate_ref[0, 0] * y
    for o_ref, (a, b) in zip(out_refs, splits):
        o_ref[...] = y[:, a:b].astype(o_ref.dtype)


def _proj(x, w, *, ntl, mods=None, norm_g=None, mod_idx=None, rms_segs=None, rms_g=None,
          w2=None, cos=None, sin=None, colscale=None, resid=None, gate_idx=None,
          splits=None, out_dtypes=None, name="proj"):
    t, kdim = x.shape
    n = w.shape[1]
    nt = t // TB
    splits = splits or [(0, n)]
    out_dtypes = out_dtypes or [BF16] * len(splits)

    def modspec(k):
        dm = mods.shape[-1]
        return pl.BlockSpec((1, 1, 1, dm), lambda i: (k, i // ntl, 0, 0))

    args, specs = [x], [pl.BlockSpec((TB, kdim), lambda i: (i, 0))]
    if norm_g is not None:
        args += [norm_g.reshape(1, kdim), mods, mods]
        specs += [pl.BlockSpec((1, kdim), lambda i: (0, 0)), modspec(mod_idx[0]), modspec(mod_idx[1])]
    args.append(w)
    specs.append(pl.BlockSpec((kdim, n), lambda i: (0, 0)))
    if rms_segs:
        args.append(rms_g.reshape(1, n))
        specs.append(pl.BlockSpec((1, n), lambda i: (0, 0)))
    rope_cols = 0
    if w2 is not None:
        rope_cols = w2.shape[1]
        args += [w2, cos, sin]
        specs += [pl.BlockSpec((kdim, rope_cols), lambda i: (0, 0)),
                  pl.BlockSpec((TB, LANES), lambda i: (i, 0)),
                  pl.BlockSpec((TB, LANES), lambda i: (i, 0))]
    if colscale is not None:
        args.append(colscale.reshape(1, n))
        specs.append(pl.BlockSpec((1, n), lambda i: (0, 0)))
    if resid is not None:
        args += [resid, mods]
        specs += [pl.BlockSpec((TB, n), lambda i: (i, 0)), modspec(gate_idx)]
    out_shape = [jax.ShapeDtypeStruct((t, b - a), dt) for (a, b), dt in zip(splits, out_dtypes)]
    out_specs = [pl.BlockSpec((TB, b - a), lambda i: (i, 0)) for (a, b) in splits]
    kern = functools.partial(
        _proj_kernel, normmod=norm_g is not None, rms_segs=rms_segs, rope_cols=rope_cols,
        colscale=colscale is not None, resid=resid is not None, splits=splits)
    outs = pl.pallas_call(
        kern, grid=(nt,), in_specs=specs, out_specs=out_specs, out_shape=out_shape,
        compiler_params=_cparams("arbitrary"), name=name,
    )(*args)
    return outs


def _softmax_step(st, vt, carry):
    m, l, acc = carry
    m_new = jnp.maximum(m, jnp.max(st, axis=0, keepdims=True))
    alpha = jnp.exp2(m - m_new)
    p = jnp.exp2(st - m_new)
    l = alpha * l + jnp.sum(p, axis=0, keepdims=True)
    acc = alpha * acc + jnp.dot(vt, p.astype(BF16), preferred_element_type=F32)
    return m_new, l, acc


def _softmax_init(dv):
    return (jnp.full((1, TB), NEG, F32), jnp.zeros((1, TB), F32), jnp.zeros((dv, TB), F32))


def _softmax_finish(carry, sink=None):
    m, l, acc = carry
    if sink is not None:
        m_f = jnp.maximum(m, sink)
        a = jnp.exp2(m - m_f)
        l = l * a + jnp.exp2(sink - m_f)
        acc = acc * a
    return (acc / l).T


def _scores_t(k, qt):
    return jnp.dot(k, qt, preferred_element_type=F32)


def _attn_kernel(*refs, kind, dk, dv, ntl):
    if kind == "na":
        q_ref, k_ref, vt_ref, bias_ref, o_ref = refs
    elif kind == "swa":
        sink_ref, q_ref, k_ref, vt_ref, o_ref = refs
    else:
        q_ref, k_ref, vt_ref, o_ref, *scratch = refs
    hp = pl.program_id(0)
    i = pl.program_id(1)
    qs = [q_ref[hh * dk:(hh + 1) * dk, :] for hh in range(2)]

    def scores(hh, kt):
        start = pl.multiple_of(kt * TB, TB)
        return _scores_t(k_ref[pl.ds(start, TB), hh * dk:(hh + 1) * dk], qs[hh])

    carry = [_softmax_init(dv) for _ in range(2)]
    if kind == "mla":
        sa_ref, sb_ref = scratch

        def half_step(kt, cur_ref, nxt_ref, states):
            for hh in range(2):
                nxt_ref[hh] = scores(hh, kt + 1)
            return tuple(_softmax_step(cur_ref[hh], vt_ref[hh, kt], states[hh]) for hh in range(2))

        def body(j, states):
            states = half_step(2 * j, sa_ref, sb_ref, states)
            return half_step(2 * j + 1, sb_ref, sa_ref, states)

        for hh in range(2):
            sa_ref[hh] = scores(hh, 0)
        carry = lax.fori_loop(0, ntl // 2, body, tuple(carry))
        carry = [_softmax_step(sa_ref[hh], vt_ref[hh, ntl], carry[hh]) for hh in range(2)]
    else:
        ts = jnp.clip(i - 1, 0, ntl - 3)
        tiles = [ts, ts + 1, ts + 2, ntl]

        def masked_scores(hh, s):
            st = scores(hh, tiles[s])
            if s == 3:
                return st
            if kind == "swa":
                kpos = tiles[s] * TB + lax.broadcasted_iota(jnp.int32, (TB, TB), 0)
                qpos = i * TB + lax.broadcasted_iota(jnp.int32, (TB, TB), 1)
                return jnp.where(jnp.abs(kpos - qpos) <= SWA_WINDOW, st, NEG)
            return st + bias_ref[0, hh, s * TB:(s + 1) * TB, :]

        sts = [masked_scores(hh, 0) for hh in range(2)]
        for s in range(4):
            nxt = [masked_scores(hh, s + 1) for hh in range(2)] if s < 3 else None
            for hh in range(2):
                carry[hh] = _softmax_step(sts[hh], vt_ref[hh, tiles[s]], carry[hh])
            sts = nxt
    outs = []
    for hh in range(2):
        sink = sink_ref[2 * hp + hh] if kind == "swa" else None
        outs.append(_softmax_finish(carry[hh], sink))
    o_ref[...] = jnp.concatenate(outs, axis=1).astype(o_ref.dtype)


def _attn_ctx_kernel(*refs, dk, dv, has_sink):
    if has_sink:
        sink_ref, q_ref, k_ref, vt_ref, _, o_ref = refs
    else:
        q_ref, k_ref, vt_ref, _, o_ref = refs
    hp = pl.program_id(0)
    outs = []
    for hh in range(2):
        st = _scores_t(k_ref[:, hh * dk:(hh + 1) * dk], q_ref[hh * dk:(hh + 1) * dk, :])
        carry = _softmax_step(st, vt_ref[hh, 0], _softmax_init(dv))
        outs.append(_softmax_finish(carry, sink_ref[2 * hp + hh] if has_sink else None))
    o_ref[...] = jnp.concatenate(outs, axis=1).astype(o_ref.dtype)


def _attn(kind, q, k, v, *, dk, dv, ntl, bias=None, sink=None):
    t = q.shape[0]
    nt = t // TB
    nh = q.shape[1] // dk
    vt = jnp.transpose(v.reshape(nt, TB, nh, dv), (2, 0, 3, 1))
    kern = functools.partial(_attn_kernel, kind=kind, dk=dk, dv=dv, ntl=ntl)
    in_specs = [
        pl.BlockSpec((2 * dk, TB), lambda hp, i, *_: (hp, i)),
        pl.BlockSpec((t, 2 * dk), lambda hp, i, *_: (0, hp)),
        pl.BlockSpec((2, nt, dv, TB), lambda hp, i, *_: (hp, 0, 0, 0)),
    ]
    assert ntl % 2 == 0
    qt = q.T
    args = [qt, k, vt]
    nsp = 0
    if kind == "na":
        def bias_map(hp, i):
            pat = jnp.where(i == 0, 0, jnp.where(i >= ntl - 1, 2, 1))
            return (pat, hp, 0, 0)
        in_specs.append(pl.BlockSpec((1, 2, 3 * TB, TB), bias_map))
        args.append(bias)
    if kind == "swa":
        nsp = 1
        args = [sink] + args
    o = pl.pallas_call(
        kern,
        grid_spec=pltpu.PrefetchScalarGridSpec(
            num_scalar_prefetch=nsp, grid=(nh // 2, ntl), in_specs=in_specs,
            out_specs=pl.BlockSpec((TB, 2 * dv), lambda hp, i, *_: (i, hp)),
            scratch_shapes=[pltpu.VMEM((2, TB, TB), F32)] * 2 if kind == "mla" else []),
        out_shape=jax.ShapeDtypeStruct((t, nh * dv), BF16),
        compiler_params=_cparams("arbitrary", "arbitrary"), name="attn_" + kind,
    )(*args)
    has_sink = kind == "swa"
    ckern = functools.partial(_attn_ctx_kernel, dk=dk, dv=dv, has_sink=has_sink)
    cargs = ([sink] if has_sink else []) + [qt, k, vt, o]
    return pl.pallas_call(
        ckern,
        grid_spec=pltpu.PrefetchScalarGridSpec(
            num_scalar_prefetch=nsp, grid=(nh // 2,),
            in_specs=[pl.BlockSpec((2 * dk, TB), lambda hp, *_: (hp, ntl)),
                      pl.BlockSpec((TB, 2 * dk), lambda hp, *_: (ntl, hp)),
                      pl.BlockSpec((2, 1, dv, TB), lambda hp, *_: (hp, ntl, 0, 0)),
                      pl.BlockSpec(memory_space=pl.ANY)],
            out_specs=pl.BlockSpec((TB, 2 * dv), lambda hp, *_: (ntl, hp))),
        out_shape=jax.ShapeDtypeStruct((t, nh * dv), BF16),
        input_output_aliases={len(cargs) - 1: 0},
        compiler_params=_cparams("arbitrary"), name="attn_ctx_" + kind,
    )(*cargs)


def _router_kernel(x_ref, g_ref, sh_ref, sc_ref, rt_ref, h_ref, aff_ref):
    h = _normmod(x_ref[...], g_ref[...], sh_ref[0, 0], sc_ref[0, 0])
    h_ref[...] = h.astype(BF16)
    lg = lax.dot_general(rt_ref[...], h, (((1,), (1,)), ((), ())),
                         precision=HIGHEST, preferred_element_type=F32)
    lg = lg - jnp.max(lg, axis=0, keepdims=True)
    e = jnp.exp(lg)
    aff_ref[...] = e / jnp.sum(e, axis=0, keepdims=True)


def _router(xa, g, mods, router_t, ntl):
    t, d = xa.shape
    ne = router_t.shape[0]

    def modspec(k):
        return pl.BlockSpec((1, 1, 1, d), lambda i: (k, i // ntl, 0, 0))

    return pl.pallas_call(
        _router_kernel, grid=(t // TB,),
        in_specs=[pl.BlockSpec((TB, d), lambda i: (i, 0)),
                  pl.BlockSpec((1, d), lambda i: (0, 0)),
                  modspec(3), modspec(4),
                  pl.BlockSpec((ne, d), lambda i: (0, 0))],
        out_specs=[pl.BlockSpec((TB, d), lambda i: (i, 0)),
                   pl.BlockSpec((ne, TB), lambda i: (0, i))],
        out_shape=[jax.ShapeDtypeStruct((t, d), BF16), jax.ShapeDtypeStruct((ne, t), F32)],
        compiler_params=_cparams("arbitrary"), name="router",
    )(xa, g.reshape(1, d), mods, mods, router_t)


def _topk_kernel(aff_ref, pos_ref, gate_ref, off_ref, *, cap, base):
    x = aff_ref[...]
    ne, nc, _ = x.shape
    bits = lax.bitcast_convert_type(x, jnp.int32)

    def count(mask):
        c = jnp.sum(jnp.where(mask, 1.0, 0.0), axis=1, keepdims=True)
        return jnp.sum(c, axis=2, keepdims=True)

    thr = jnp.zeros((ne, 1, 1), jnp.int32)
    for b in range(30, -1, -1):
        cand = thr | (1 << b)
        thr = jnp.where(count(bits >= cand) >= cap, cand, thr)

    ia = lax.broadcasted_iota(jnp.int32, (LANES, LANES), 0)
    ib = lax.broadcasted_iota(jnp.int32, (LANES, LANES), 1)
    upper = jnp.where(ia <= ib, 1.0, 0.0)
    ones = jnp.ones((LANES, LANES), F32)
    ca = lax.broadcasted_iota(jnp.int32, (nc, nc), 0)
    cb = lax.broadcasted_iota(jnp.int32, (nc, nc), 1)
    lower = jnp.where(cb < ca, 1.0, 0.0)

    def prefix(mask):
        m2 = jnp.where(mask, 1.0, 0.0).reshape(ne * nc, LANES)
        within = jnp.dot(m2, upper, precision=HIGHEST, preferred_element_type=F32)
        tot = jnp.dot(m2, ones, precision=HIGHEST, preferred_element_type=F32).reshape(ne, nc, LANES)
        offs = [jnp.dot(lower, tot[e], precision=HIGHEST, preferred_element_type=F32)[None]
                for e in range(ne)]
        off = jnp.concatenate(offs, axis=0)
        return within.reshape(ne, nc, LANES) + off, off

    gt = bits > thr
    eq = bits == thr
    need = cap - count(gt)
    eq_rank, _ = prefix(eq)
    sel = gt | (eq & (eq_rank <= need))
    sel_rank, off = prefix(sel)
    pos_ref[...] = jnp.where(sel, sel_rank - 1.0 + base, -1.0)
    gate_ref[...] = jnp.where(sel, x, 0.0)
    off_ref[...] = off.astype(jnp.int32) + base


def _topk(aff3, cap, base):
    ne, nc, _ = aff3.shape
    kern = functools.partial(_topk_kernel, cap=cap, base=base)
    spec = pl.BlockSpec((ne, nc, LANES), lambda i: (0, 0, 0))
    return pl.pallas_call(
        kern, grid=(1,), in_specs=[spec], out_specs=[spec, spec, spec],
        out_shape=[jax.ShapeDtypeStruct(aff3.shape, F32), jax.ShapeDtypeStruct(aff3.shape, F32),
                   jax.ShapeDtypeStruct(aff3.shape, jnp.int32)],
        compiler_params=_cparams("arbitrary"), name="topk",
    )(aff3)


def _ffn_kernel(offs_ref, pos_ref, gate_ref, h_ref, wg_ref, wu_ref, wd_ref, y_ref,
                xs_ref, xb_ref, gs_ref, *, ng, gt, nf, nrows):
    e = pl.program_id(0)
    j = pl.program_id(1)

    @pl.when(j == 0)
    def _():
        xs_ref[...] = jnp.zeros_like(xs_ref)
        gs_ref[...] = jnp.zeros_like(gs_ref)

    @pl.when(j < ng)
    def _():
        for s in range(gt):
            off = offs_ref[e, j * gt + s]
            w0 = pl.multiple_of((off // 8) * 8, 8)
            prow = pos_ref[0, :, s * TB:(s + 1) * TB]
            grow = gate_ref[0, :, s * TB:(s + 1) * TB]
            ids = (w0 + lax.broadcasted_iota(jnp.int32, (GATHER_WIN, 1), 0)).astype(F32)
            hit = prow == ids
            onehot = jnp.where(hit, 1.0, 0.0).astype(BF16)
            xs_ref[pl.ds(w0, GATHER_WIN), :] += jnp.dot(
                onehot, h_ref[s * TB:(s + 1) * TB, :], preferred_element_type=F32)
            gsel = jnp.sum(jnp.where(hit, grow, 0.0), axis=1, keepdims=True)
            gs_ref[pl.ds(w0, GATHER_WIN), :] += jnp.broadcast_to(gsel, (GATHER_WIN, LANES))

    @pl.when(j == ng)
    def _():
        xb_ref[...] = xs_ref[0:nrows, :].astype(BF16)
        xs_ref[...] = jnp.zeros_like(xs_ref)

    @pl.when(j >= ng)
    def _():
        wg = wg_ref[0].astype(BF16)
        wu = wu_ref[0].astype(BF16)
        wd = wd_ref[0].astype(BF16)
        for b in range(nrows // FFN_ROWS):
            rows = slice(b * FFN_ROWS, (b + 1) * FFN_ROWS)
            x = xb_ref[rows, :]
            a = jnp.dot(x, wg, preferred_element_type=F32)
            u = jnp.dot(x, wu, preferred_element_type=F32)
            hmid = (a * jax.nn.sigmoid(a) * u).astype(BF16)
            xs_ref[rows, :] += jnp.dot(hmid, wd, preferred_element_type=F32)

    @pl.when(j == ng + nf - 1)
    def _():
        y_ref[0, 0:nrows, :] = (xs_ref[0:nrows, :] * gs_ref[0:nrows, 0:1]).astype(y_ref.dtype)
        if y_ref.shape[1] > nrows:
            y_ref[0, nrows:, :] = jnp.zeros((y_ref.shape[1] - nrows, y_ref.shape[2]), y_ref.dtype)


def _ffn(offs, pos, gate, h, wg, wu, wd, *, nslots):
    t, d = h.shape
    nt = t // TB
    ne, _, f = wg.shape
    gt = max(g for g in range(1, 9) if nt % g == 0)
    ng = nt // gt
    nf = f // FFN_FCHUNK
    nrows = -(-nslots // FFN_ROWS) * FFN_ROWS
    srows = -(-(nslots + TB) // 16) * 16
    srows = max(srows, nrows)
    xs_rows = max(nrows, nslots + GATHER_WIN)
    kern = functools.partial(_ffn_kernel, ng=ng, gt=gt, nf=nf, nrows=nrows)

    def fchunk(j):
        return jnp.clip(j - ng, 0, nf - 1)

    grid_spec = pltpu.PrefetchScalarGridSpec(
        num_scalar_prefetch=1, grid=(ne, ng + nf),
        in_specs=[
            pl.BlockSpec((1, 1, gt * TB), lambda e, j, o: (e, 0, jnp.minimum(j, ng - 1))),
            pl.BlockSpec((1, 1, gt * TB), lambda e, j, o: (e, 0, jnp.minimum(j, ng - 1))),
            pl.BlockSpec((gt * TB, d), lambda e, j, o: (jnp.minimum(j, ng - 1), 0)),
            pl.BlockSpec((1, d, FFN_FCHUNK), lambda e, j, o: (e, 0, fchunk(j))),
            pl.BlockSpec((1, d, FFN_FCHUNK), lambda e, j, o: (e, 0, fchunk(j))),
            pl.BlockSpec((1, FFN_FCHUNK, d), lambda e, j, o: (e, fchunk(j), 0)),
        ],
        out_specs=pl.BlockSpec((1, srows, d), lambda e, j, o: (e, 0, 0)),
        scratch_shapes=[pltpu.VMEM((xs_rows, d), F32), pltpu.VMEM((nrows, d), BF16),
                        pltpu.VMEM((xs_rows, LANES), F32)])
    return pl.pallas_call(
        kern, grid_spec=grid_spec,
        out_shape=jax.ShapeDtypeStruct((ne, srows, d), BF16),
        compiler_params=_cparams("arbitrary", "arbitrary"), name="ffn",
    )(offs, pos.reshape(ne, 1, t), gate.reshape(ne, 1, t), h, wg, wu, wd)


def _combine_kernel(*refs, ne, srows):
    offs_ref, cnt_ref = refs[0], refs[1]
    posc_ref, xa_ref, gate_ref = refs[2], refs[3], refs[4]
    win_refs = refs[5:5 + ne]
    win2_refs = refs[5 + ne:5 + 2 * ne]
    o_ref = refs[5 + 2 * ne]
    j = pl.program_id(0)
    lane = lax.broadcasted_iota(jnp.int32, (1, TB), 1)
    acc = None
    for e in range(ne):
        w0 = (offs_ref[e, j] // 16) * 16
        pcol = posc_ref[:, e:e + 1]
        onehot = jnp.where(pcol == (w0 + lane).astype(F32), 1.0, 0.0).astype(BF16)
        part = jnp.dot(onehot, win_refs[e][...], preferred_element_type=F32)
        acc = part if acc is None else acc + part
    o_ref[...] = xa_ref[...] + gate_ref[0, 0] * acc

    for e in range(ne):
        off = offs_ref[e, j]
        w0 = (off // 16) * 16

        @pl.when(off - w0 + cnt_ref[e, j] > TB)
        def _(e=e, w0=w0):
            w1 = jnp.minimum(w0 + TB, srows - TB)
            ids = w1 + lane
            hit = (posc_ref[:, e:e + 1] == ids.astype(F32)) & (ids >= w0 + TB)
            oh = jnp.where(hit, 1.0, 0.0).astype(BF16)
            o_ref[...] += gate_ref[0, 0] * jnp.dot(oh, win2_refs[e][...], preferred_element_type=F32)


def _combine(offs, cnts, pos_cols, xa, mods, ys, ntl):
    t, d = xa.shape
    nt = t // TB
    ne, srows, _ = ys.shape
    assert srows % 16 == 0
    ys2 = ys.reshape(ne * srows, d)

    def win_spec(e):
        def imap(j, offs, cnts):
            return ((e * srows // 16 + offs[e, j] // 16) * 16, 0)
        return pl.BlockSpec((pl.Element(TB), pl.Element(d)), imap)

    def win2_spec(e):
        def imap(j, offs, cnts):
            off = offs[e, j]
            w0 = (off // 16) * 16
            need = off - w0 + cnts[e, j] > TB
            w1 = jnp.where(need, jnp.minimum(w0 + TB, srows - TB), 0)
            return ((e * srows // 16 + w1 // 16) * 16, 0)
        return pl.BlockSpec((pl.Element(TB), pl.Element(d)), imap)

    in_specs = [
        pl.BlockSpec((TB, ne), lambda j, o, c: (j, 0)),
        pl.BlockSpec((TB, d), lambda j, o, c: (j, 0)),
        pl.BlockSpec((1, 1, 1, d), lambda j, o, c: (5, j // ntl, 0, 0)),
    ] + [win_spec(e) for e in range(ne)] + [win2_spec(e) for e in range(ne)]
    grid_spec = pltpu.PrefetchScalarGridSpec(
        num_scalar_prefetch=2, grid=(nt,), in_specs=in_specs,
        out_specs=pl.BlockSpec((TB, d), lambda j, o, c: (j, 0)))
    kern = functools.partial(_combine_kernel, ne=ne, srows=srows)
    return pl.pallas_call(
        kern, grid_spec=grid_spec, out_shape=jax.ShapeDtypeStruct((t, d), F32),
        compiler_params=_cparams("arbitrary"), name="combine",
    )(offs, cnts, pos_cols, xa, mods, *([ys2] * (2 * ne)))


def _final_kernel(x_ref, g_ref, o_ref):
    x = x_ref[...]
    o_ref[...] = x * lax.rsqrt(jnp.mean(x * x, axis=-1, keepdims=True) + RMS_EPS) * g_ref[...]


def _final_norm(xa, g, seq):
    d = xa.shape[1]
    return pl.pallas_call(
        _final_kernel, grid=(seq // TB,),
        in_specs=[pl.BlockSpec((TB, d), lambda i: (i, 0)), pl.BlockSpec((1, d), lambda i: (0, 0))],
        out_specs=pl.BlockSpec((TB, d), lambda i: (i, 0)),
        out_shape=jax.ShapeDtypeStruct((seq, d), F32),
        compiler_params=_cparams("arbitrary"), name="final_norm",
    )(xa, g.reshape(1, d))


def _rope_tables(seq, t, rot_dim, lane_layout):
    quarter = rot_dim // 4
    inv_freq = ROPE_THETA ** (-jnp.arange(quarter, dtype=F32) / quarter)
    tok = jnp.arange(seq)
    row = (tok // GRID_W).astype(F32)[:, None] * inv_freq
    col = (tok % GRID_W).astype(F32)[:, None] * inv_freq
    cos_blk = jnp.concatenate([jnp.cos(row), jnp.cos(row), jnp.cos(col), jnp.cos(col)], axis=1)
    sin_blk = jnp.concatenate([jnp.sin(row), jnp.sin(row), jnp.sin(col), jnp.sin(col)], axis=1)
    cos = jnp.ones((seq, LANES), F32)
    sin = jnp.zeros((seq, LANES), F32)
    for lo, hi in lane_layout:
        cos = cos.at[:, lo:hi].set(cos_blk)
        sin = sin.at[:, lo:hi].set(sin_blk)
    cos = jnp.concatenate([cos, jnp.ones((t - seq, LANES), F32)], axis=0)
    sin = jnp.concatenate([sin, jnp.zeros((t - seq, LANES), F32)], axis=0)
    return cos, sin


def _rope_partner_perm(rot_dim):
    q = rot_dim // 4
    src = np.concatenate([np.arange(q, 2 * q), np.arange(0, q), np.arange(3 * q, 4 * q), np.arange(2 * q, 3 * q)])
    sign = np.concatenate([-np.ones(q), np.ones(q), -np.ones(q), np.ones(q)]).astype(np.float32)
    return src, sign


def _na_bias(rpb, rows, ntl):
    nh = rpb.shape[0]
    rpt = TB // GRID_W
    kr = min(NA_WIN_ROWS, rows)
    c = np.arange(GRID_W)
    cs = np.clip(c - NA_WIN_COLS // 2, 0, GRID_W - NA_WIN_COLS)
    okc = (c[:, None] >= cs[None, :]) & (c[:, None] < cs[None, :] + NA_WIN_COLS)
    dc = np.clip(c[:, None] - c[None, :] + NA_WIN_COLS - 1, 0, 2 * NA_WIN_COLS - 2)
    toep = jnp.where(okc[None, None], rpb[:, :, dc] * LOG2E, NEG)
    neg = jnp.full((nh, GRID_W, GRID_W), NEG, F32)
    out = []
    for i in (0, 1, ntl - 1):
        ts = min(max(i - 1, 0), ntl - 3)
        key_rows = []
        for krow in range(3 * rpt):
            r2 = ts * rpt + krow
            blocks = []
            for qrow in range(rpt):
                r = i * rpt + qrow
                rs = min(max(r - kr // 2, 0), rows - kr)
                blocks.append(toep[:, r2 - r + NA_WIN_ROWS - 1] if rs <= r2 < rs + kr else neg)
            key_rows.append(jnp.concatenate(blocks, axis=2))
        out.append(jnp.concatenate(key_rows, axis=1))
    return jnp.stack(out, axis=0).astype(F32)


def _mixer_na(xa, g, mods, w_qkv, w_o, rpb, ntl, seq):
    d = xa.shape[1]
    hd = d // NA_HEADS
    n = NA_HEADS * hd
    scale = hd ** -0.5 * LOG2E
    colscale = jnp.concatenate([jnp.full((n,), scale, F32), jnp.ones((2 * n,), F32)])
    q, k, v = _proj(xa, w_qkv.astype(BF16), ntl=ntl, mods=mods, norm_g=g, mod_idx=(0, 1),
                    colscale=colscale, splits=[(0, n), (n, 2 * n), (2 * n, 3 * n)], name="na_qkv")
    bias = _na_bias(rpb, seq // GRID_W, ntl)
    o = _attn("na", q, k, v, dk=hd, dv=hd, ntl=ntl, bias=bias)
    (xa,) = _proj(o, w_o.astype(BF16), ntl=ntl, mods=mods, resid=xa, gate_idx=2,
                  out_dtypes=[F32], name="na_out")
    return xa


def _mixer_swa(xa, g, mods, w_qkv, w_o, sink, ntl, seq):
    t, d = xa.shape
    hd = d // SWA_Q_HEADS
    nq, nkv = SWA_Q_HEADS * hd, SWA_KV_HEADS * hd
    group = SWA_Q_HEADS // SWA_KV_HEADS
    scale = hd ** -0.5 * LOG2E
    wq = w_qkv[:, :nq]
    wk = jnp.repeat(w_qkv[:, nq:nq + nkv].reshape(d, SWA_KV_HEADS, hd), group, axis=1).reshape(d, nq)
    wv = jnp.repeat(w_qkv[:, nq + nkv:].reshape(d, SWA_KV_HEADS, hd), group, axis=1).reshape(d, nq)
    w = jnp.concatenate([wq, wk, wv], axis=1)
    src, sign = _rope_partner_perm(hd)
    nblk = 2 * nq // hd
    src_full = (np.arange(nblk)[:, None] * hd + src[None, :]).reshape(-1)
    sign_full = np.tile(sign, nblk)
    w2 = w[:, src_full] * sign_full[None, :]
    cos, sin = _rope_tables(seq, t, hd, [(0, hd), (hd, 2 * hd)])
    colscale = jnp.concatenate([jnp.full((nq,), scale, F32), jnp.ones((2 * nq,), F32)])
    q, k, v = _proj(xa, w.astype(BF16), ntl=ntl, mods=mods, norm_g=g, mod_idx=(0, 1),
                    w2=w2.astype(BF16), cos=cos, sin=sin, colscale=colscale,
                    splits=[(0, nq), (nq, 2 * nq), (2 * nq, 3 * nq)], name="swa_qkv")
    o = _attn("swa", q, k, v, dk=hd, dv=hd, ntl=ntl, sink=sink.astype(F32) * LOG2E)
    (xa,) = _proj(o, w_o.astype(BF16), ntl=ntl, mods=mods, resid=xa, gate_idx=2,
                  out_dtypes=[F32], name="swa_out")
    return xa


def _mixer_mla(xa, g, mods, w_dq, q_norm, w_uq, w_dkv, kv_norm, w_ukv, w_o, ntl, seq):
    t, d = xa.shape
    nh = MLA_HEADS
    qr = w_dq.shape[1]
    kvr = kv_norm.shape[0]
    nope, rope, dv = MLA_NOPE_DIM, MLA_ROPE_DIM, MLA_V_DIM
    qk = nope + rope
    pad = LANES - qk
    scale = qk ** -0.5 * LOG2E
    kin = kvr + LANES
    w_dn = jnp.concatenate([w_dq, w_dkv, jnp.zeros((d, LANES - rope), F32)], axis=1)
    g_dn = jnp.concatenate([q_norm, kv_norm, jnp.ones((LANES,), F32)])
    cqn, kvin = _proj(xa, w_dn.astype(BF16), ntl=ntl, mods=mods, norm_g=g, mod_idx=(0, 1),
                      rms_segs=[(0, qr), (qr, qr + kvr)], rms_g=g_dn,
                      splits=[(0, qr), (qr, qr + kin)], name="mla_down")
    src, sign = _rope_partner_perm(rope)
    src_head = np.concatenate([np.arange(nope), nope + src, np.arange(qk, LANES)])
    sign_head = np.concatenate([np.zeros(nope, np.float32), sign, np.zeros(pad, np.float32)])
    src_full = (np.arange(nh)[:, None] * LANES + src_head[None, :]).reshape(-1)
    sign_full = np.tile(sign_head, nh)
    cos, sin = _rope_tables(seq, t, rope, [(nope, qk)])
    wq = jnp.pad(w_uq.reshape(qr, nh, qk), ((0, 0), (0, 0), (0, pad))).reshape(qr, nh * LANES)
    wq2 = wq[:, src_full] * sign_full[None, :]
    (q,) = _proj(cqn, wq.astype(BF16), ntl=ntl, w2=wq2.astype(BF16), cos=cos, sin=sin,
                 colscale=jnp.full((nh * LANES,), scale, F32), name="mla_q")
    w_ukv3 = w_ukv.reshape(kvr, nh, nope + dv)
    wk_top = jnp.pad(w_ukv3[:, :, :nope], ((0, 0), (0, 0), (0, LANES - nope)))
    eye = jnp.eye(rope, dtype=F32)
    wk_rope = jnp.pad(eye, ((0, LANES - rope), (nope, pad)))
    wk_bot = jnp.broadcast_to(wk_rope[:, None, :], (LANES, nh, LANES))
    wk = jnp.concatenate([wk_top, wk_bot], axis=0).reshape(kin, nh * LANES)
    wk2 = wk[:, src_full] * sign_full[None, :]
    wv = jnp.concatenate([w_ukv3[:, :, nope:].reshape(kvr, nh * dv), jnp.zeros((LANES, nh * dv), F32)], axis=0)
    (k,) = _proj(kvin, wk.astype(BF16), ntl=ntl, w2=wk2.astype(BF16), cos=cos, sin=sin, name="mla_k")
    (v,) = _proj(kvin, wv.astype(BF16), ntl=ntl, name="mla_v")
    o = _attn("mla", q, k, v, dk=LANES, dv=dv, ntl=ntl)
    (xa,) = _proj(o, w_o.astype(BF16), ntl=ntl, mods=mods, resid=xa, gate_idx=2,
                  out_dtypes=[F32], name="mla_out")
    return xa


def _moe(xa, g, mods, router, w_gate, w_up, w_down, ntl, seq, with_ctx):
    t, d = xa.shape
    ne = router.shape[1]
    nctx = t - seq
    h, aff = _router(xa, g, mods, router.T, ntl)
    cap_l = EC_CAPACITY_FACTOR * seq // ne
    pos_l, gate_l, off_l = _topk(aff[:, :seq].reshape(ne, seq // CHUNK, CHUNK), cap_l, 0)
    per_tile = TB // CHUNK
    pos = [pos_l.reshape(ne, seq)]
    gate = [gate_l.reshape(ne, seq)]
    offs = [off_l[:, ::per_tile, 0]]
    if with_ctx:
        cap_c = EC_CAPACITY_FACTOR * nctx // ne
        cpad = 8 * CHUNK
        aff_c = jnp.concatenate([aff[:, seq:], jnp.full((ne, cpad - nctx), -1.0, F32)], axis=1)
        pos_c, gate_c, off_c = _topk(aff_c.reshape(ne, 8, CHUNK), cap_c, cap_l)
        pos.append(pos_c.reshape(ne, cpad)[:, :nctx])
        gate.append(gate_c.reshape(ne, cpad)[:, :nctx])
        offs.append(off_c[:, 0:nctx // CHUNK:per_tile, 0])
        nslots = cap_l + cap_c
    else:
        pos.append(jnp.full((ne, nctx), -1.0, F32))
        gate.append(jnp.zeros((ne, nctx), F32))
        offs.append(jnp.full((ne, nctx // TB), cap_l, jnp.int32))
        nslots = cap_l
    pos = jnp.concatenate(pos, axis=1)
    gate = jnp.concatenate(gate, axis=1)
    offs = jnp.concatenate(offs + [jnp.full((ne, 1), nslots, jnp.int32)], axis=1)
    cnts = offs[:, 1:] - offs[:, :-1]
    ys = _ffn(offs, pos, gate, h, w_gate, w_up, w_down, nslots=nslots)
    return _combine(offs, cnts, pos.T, xa, mods, ys, ntl)


def kernel(x, c, ctx, c_ctx, ada_w, ada_b, norm_mix, norm_ffn, na_w_qkv, na_w_o, na_rpb, swa_w_qkv, swa_w_o, swa_sink, mla_w_dq, mla_q_norm, mla_w_uq, mla_w_dkv, mla_kv_norm, mla_w_ukv, mla_w_o, moe_router, moe_w_gate, moe_w_up, moe_w_down, final_norm):
    assert x.shape[0] == 1 and c.shape[0] == 1 and ctx.shape[0] == 1
    seq, d = x.shape[1], x.shape[2]
    nctx = ctx.shape[1]
    assert seq % TB == 0 and nctx == TB and seq // TB >= 4
    depth = ada_w.shape[0]
    ntl = seq // TB
    xa = jnp.concatenate([x[0], ctx[0]], axis=0)
    cs = jnp.concatenate([c, c_ctx[None, :], jnp.zeros((6, d), F32)], axis=0)
    ada = _ada(cs, ada_w, ada_b)
    mods_all = jnp.transpose(ada[:, :2].reshape(depth, 2, 6, d), (0, 2, 1, 3))[:, :, :, None, :]
    for i in range(depth):
        mods = mods_all[i]
        kind, slot = i % N_MIXERS, i // N_MIXERS
        if kind == 0:
            xa = _mixer_na(xa, norm_mix[i], mods, na_w_qkv[slot], na_w_o[slot], na_rpb[slot], ntl, seq)
        elif kind == 1:
            xa = _mixer_swa(xa, norm_mix[i], mods, swa_w_qkv[slot], swa_w_o[slot], swa_sink[slot], ntl, seq)
        else:
            xa = _mixer_mla(xa, norm_mix[i], mods, mla_w_dq[slot], mla_q_norm[slot], mla_w_uq[slot],
                            mla_w_dkv[slot], mla_kv_norm[slot], mla_w_ukv[slot], mla_w_o[slot], ntl, seq)
        xa = _moe(xa, norm_ffn[i], mods, moe_router[i], moe_w_gate[i], moe_w_up[i], moe_w_down[i],
                  ntl, seq, with_ctx=i < depth - 1)
    return _final_norm(xa, final_norm, seq)[None]
```

```python
import functools
import math

import jax
import jax.numpy as jnp
import numpy as np
from jax import lax
from jax.experimental import pallas as pl
from jax.experimental.pallas import tpu as pltpu

F32 = jnp.float32
BF16 = jnp.bfloat16
HIGHEST = lax.Precision.HIGHEST

GRID_W = 64
N_MIXERS = 3
RMS_EPS = 1e-6
ROPE_THETA = 10000.0
NA_HEADS = 16
NA_WIN_ROWS = 8
NA_WIN_COLS = 16
SWA_Q_HEADS = 16
SWA_KV_HEADS = 4
SWA_WINDOW = 128
MLA_HEADS = 16
MLA_NOPE_DIM = 64
MLA_ROPE_DIM = 32
MLA_V_DIM = 64
N_EXPERTS = 16
EC_CAPACITY_FACTOR = 2

TB = 256
CHUNK = 128
LANES = 128
NEG = -1e30
LOG2E = math.log2(math.e)
FFN_ROWS = 416
FFN_FCHUNK = 512
GATHER_WIN = TB + 16
VMEM_LIMIT = 56 * 1024 * 1024


def _cparams(*sem):
    return pltpu.CompilerParams(dimension_semantics=sem, vmem_limit_bytes=VMEM_LIMIT)


def _ada_kernel(cs_ref, w_ref, b_ref, o_ref):
    x = cs_ref[...]
    x = x * jax.nn.sigmoid(x)
    y = jnp.dot(x, w_ref[0], precision=HIGHEST, preferred_element_type=F32)
    o_ref[0] = y + b_ref[0]


def _ada(cs, ada_w, ada_b):
    depth, d, n = ada_w.shape
    nb = n // d
    return pl.pallas_call(
        _ada_kernel,
        grid=(depth, nb),
        in_specs=[
            pl.BlockSpec((8, d), lambda i, k: (0, 0)),
            pl.BlockSpec((1, d, d), lambda i, k: (i, 0, k)),
            pl.BlockSpec((1, 1, d), lambda i, k: (i, 0, k)),
        ],
        out_specs=pl.BlockSpec((1, 8, d), lambda i, k: (i, 0, k)),
        out_shape=jax.ShapeDtypeStruct((depth, 8, n), F32),
        compiler_params=_cparams("arbitrary", "arbitrary"),
        name="ada",
    )(cs, ada_w, ada_b.reshape(depth, 1, n))


def _normmod(x, g, shift, scale):
    y = x * lax.rsqrt(jnp.mean(x * x, axis=-1, keepdims=True) + RMS_EPS)
    return (y * g) * (1.0 + scale) + shift


def _proj_kernel(*refs, normmod, rms_segs, rope_cols, colscale, resid, splits):
    it = iter(refs)
    x_ref = next(it)
    if normmod:
        g_ref, sh_ref, sc_ref = next(it), next(it), next(it)
    w_ref = next(it)
    if rms_segs:
        g2_ref = next(it)
    if rope_cols:
        w2_ref, cos_ref, sin_ref = next(it), next(it), next(it)
    if colscale:
        cs_ref = next(it)
    if resid:
        res_ref, gate_ref = next(it), next(it)
    out_refs = list(it)

    x = x_ref[...]
    if normmod:
        x = _normmod(x.astype(F32), g_ref[...], sh_ref[0, 0], sc_ref[0, 0])
    xb = x.astype(BF16)
    y = jnp.dot(xb, w_ref[...], preferred_element_type=F32)
    if rms_segs:
        parts = []
        prev = 0
        for (a, b) in rms_segs:
            if a > prev:
                parts.append(y[:, prev:a])
            seg = y[:, a:b]
            seg = seg * lax.rsqrt(jnp.mean(seg * seg, axis=-1, keepdims=True) + RMS_EPS)
            parts.append(seg * g2_ref[:, a:b])
            prev = b
        if prev < y.shape[1]:
            parts.append(y[:, prev:])
        y = jnp.concatenate(parts, axis=1)
    if rope_cols:
        y2 = jnp.dot(xb, w2_ref[...], preferred_element_type=F32)
        reps = rope_cols // LANES
        cos = jnp.tile(cos_ref[...], (1, reps))
        sin = jnp.tile(sin_ref[...], (1, reps))
        yr = y[:, :rope_cols] * cos + y2 * sin
        y = yr if rope_cols == y.shape[1] else jnp.concatenate([yr, y[:, rope_cols:]], axis=1)
    if colscale:
        y = y * cs_ref[...]
    if resid:
        y = res_ref[...] + gate_ref[0, 0] * y
    for o_ref, (a, b) in zip(out_refs, splits):
        o_ref[...] = y[:, a:b].astype(o_ref.dtype)


def _proj(x, w, *, ntl, mods=None, norm_g=None, mod_idx=None, rms_segs=None, rms_g=None,
          w2=None, cos=None, sin=None, colscale=None, resid=None, gate_idx=None,
          splits=None, out_dtypes=None, name="proj"):
    t, kdim = x.shape
    n = w.shape[1]
    nt = t // TB
    splits = splits or [(0, n)]
    out_dtypes = out_dtypes or [BF16] * len(splits)

    def modspec(k):
        dm = mods.shape[-1]
        return pl.BlockSpec((1, 1, 1, dm), lambda i: (k, i // ntl, 0, 0))

    args, specs = [x], [pl.BlockSpec((TB, kdim), lambda i: (i, 0))]
    if norm_g is not None:
        args += [norm_g.reshape(1, kdim), mods, mods]
        specs += [pl.BlockSpec((1, kdim), lambda i: (0, 0)), modspec(mod_idx[0]), modspec(mod_idx[1])]
    args.append(w)
    specs.append(pl.BlockSpec((kdim, n), lambda i: (0, 0)))
    if rms_segs:
        args.append(rms_g.reshape(1, n))
        specs.append(pl.BlockSpec((1, n), lambda i: (0, 0)))
    rope_cols = 0
    if w2 is not None:
        rope_cols = w2.shape[1]
        args += [w2, cos, sin]
        specs += [pl.BlockSpec((kdim, rope_cols), lambda i: (0, 0)),
                  pl.BlockSpec((TB, LANES), lambda i: (i, 0)),
                  pl.BlockSpec((TB, LANES), lambda i: (i, 0))]
    if colscale is not None:
        args.append(colscale.reshape(1, n))
        specs.append(pl.BlockSpec((1, n), lambda i: (0, 0)))
    if resid is not None:
        args += [resid, mods]
        specs += [pl.BlockSpec((TB, n), lambda i: (i, 0)), modspec(gate_idx)]
    out_shape = [jax.ShapeDtypeStruct((t, b - a), dt) for (a, b), dt in zip(splits, out_dtypes)]
    out_specs = [pl.BlockSpec((TB, b - a), lambda i: (i, 0)) for (a, b) in splits]
    kern = functools.partial(
        _proj_kernel, normmod=norm_g is not None, rms_segs=rms_segs, rope_cols=rope_cols,
        colscale=colscale is not None, resid=resid is not None, splits=splits)
    outs = pl.pallas_call(
        kern, grid=(nt,), in_specs=specs, out_specs=out_specs, out_shape=out_shape,
        compiler_params=_cparams("arbitrary"), name=name,
    )(*args)
    return outs


ONES_ROWS = 16


def _softmax_step(st, vt, carry):
    m, acc = carry
    m_new = jnp.maximum(m, jnp.max(st, axis=0, keepdims=True))
    alpha = jnp.exp2(m - m_new)
    p = jnp.exp2(st - m_new)
    acc = alpha * acc + jnp.dot(vt, p.astype(BF16), preferred_element_type=F32)
    return m_new, acc


def _softmax_init(dv):
    return (jnp.full((1, TB), NEG, F32), jnp.zeros((dv + ONES_ROWS, TB), F32))


def _softmax_finish(carry, dv, sink=None):
    m, acc = carry
    num, l = acc[:dv], acc[dv:dv + 1]
    if sink is not None:
        m_f = jnp.maximum(m, sink)
        a = jnp.exp2(m - m_f)
        l = l * a + jnp.exp2(sink - m_f)
        num = num * a
    return (num / l).T


def _scores_t(k, qt):
    return jnp.dot(k, qt, preferred_element_type=F32)


def _attn_kernel(*refs, kind, dk, dv, ntl, hps):
    heads = range(hps)
    if kind == "na":
        q_ref, k_ref, vt_ref, bias_ref, o_ref = refs
    elif kind == "swa":
        sink_ref, q_ref, k_ref, vt_ref, bias_ref, o_ref = refs
    else:
        q_ref, k_ref, vt_ref, o_ref, *scratch = refs
    hp = pl.program_id(0)
    i = pl.program_id(1)
    qs = [q_ref[hh * dk:(hh + 1) * dk, :] for hh in heads]

    def scores(hh, kt):
        start = pl.multiple_of(kt * TB, TB)
        return _scores_t(k_ref[pl.ds(start, TB), hh * dk:(hh + 1) * dk], qs[hh])

    carry = [_softmax_init(dv) for _ in heads]
    if kind == "mla":
        sa_ref, sb_ref = scratch

        def half_step(kt, cur_ref, nxt_ref, states):
            for hh in heads:
                nxt_ref[hh] = scores(hh, kt + 1)
            return tuple(_softmax_step(cur_ref[hh], vt_ref[hh, kt], states[hh]) for hh in heads)

        def body(j, states):
            states = half_step(2 * j, sa_ref, sb_ref, states)
            return half_step(2 * j + 1, sb_ref, sa_ref, states)

        for hh in heads:
            sa_ref[hh] = scores(hh, 0)
        carry = lax.fori_loop(0, ntl // 2, body, tuple(carry), unroll=4 if ntl % 8 == 0 else 2)
        carry = [_softmax_step(sa_ref[hh], vt_ref[hh, ntl], carry[hh]) for hh in heads]
    else:
        ts = jnp.clip(i - 1, 0, ntl - 3)
        tiles = [ts, ts + 1, ts + 2, ntl]

        def masked_scores(hh, s):
            st = scores(hh, tiles[s])
            if s == 3:
                return st
            return st + bias_ref[0, hh if kind == "na" else 0, s * TB:(s + 1) * TB, :]

        sts = [masked_scores(hh, 0) for hh in heads]
        for s in range(4):
            nxt = [masked_scores(hh, s + 1) for hh in heads] if s < 3 else None
            for hh in heads:
                carry[hh] = _softmax_step(sts[hh], vt_ref[hh, tiles[s]], carry[hh])
            sts = nxt
    outs = []
    for hh in heads:
        sink = sink_ref[hps * hp + hh] if kind == "swa" else None
        outs.append(_softmax_finish(carry[hh], dv, sink))
    o_ref[...] = jnp.concatenate(outs, axis=1).astype(o_ref.dtype)


def _attn_ctx_kernel(*refs, dk, dv, has_sink):
    if has_sink:
        sink_ref, q_ref, k_ref, vt_ref, _, o_ref = refs
    else:
        q_ref, k_ref, vt_ref, _, o_ref = refs
    hp = pl.program_id(0)
    outs = []
    for hh in range(2):
        st = _scores_t(k_ref[:, hh * dk:(hh + 1) * dk], q_ref[hh * dk:(hh + 1) * dk, :])
        carry = _softmax_step(st, vt_ref[hh, 0], _softmax_init(dv))
        outs.append(_softmax_finish(carry, dv, sink_ref[2 * hp + hh] if has_sink else None))
    o_ref[...] = jnp.concatenate(outs, axis=1).astype(o_ref.dtype)


def _attn(kind, q, k, v, *, dk, dv, ntl, bias=None, sink=None):
    t = q.shape[0]
    nt = t // TB
    nh = q.shape[1] // dk
    vt = jnp.transpose(v.reshape(nt, TB, nh, dv), (2, 0, 3, 1))
    vt = jnp.concatenate([vt, jnp.ones((nh, nt, ONES_ROWS, TB), vt.dtype)], axis=2)
    dva = dv + ONES_ROWS
    hps = 2 if kind == "mla" else 4
    kern = functools.partial(_attn_kernel, kind=kind, dk=dk, dv=dv, ntl=ntl, hps=hps)
    in_specs = [
        pl.BlockSpec((hps * dk, TB), lambda hp, i, *_: (hp, i)),
        pl.BlockSpec((t, hps * dk), lambda hp, i, *_: (0, hp)),
        pl.BlockSpec((hps, nt, dva, TB), lambda hp, i, *_: (hp, 0, 0, 0)),
    ]
    assert ntl % 2 == 0
    qt = q.T
    args = [qt, k, vt]
    nsp = 0
    if kind in ("na", "swa"):
        per_head = kind == "na"

        def bias_map(hp, i, *_):
            pat = jnp.where(i == 0, 0, jnp.where(i >= ntl - 1, 2, 1))
            return (pat, hp if per_head else 0, 0, 0)
        in_specs.append(pl.BlockSpec((1, hps if per_head else 1, 3 * TB, TB), bias_map))
        args.append(bias)
    if kind == "swa":
        nsp = 1
        args = [sink] + args
    o = pl.pallas_call(
        kern,
        grid_spec=pltpu.PrefetchScalarGridSpec(
            num_scalar_prefetch=nsp, grid=(nh // hps, ntl), in_specs=in_specs,
            out_specs=pl.BlockSpec((TB, hps * dv), lambda hp, i, *_: (i, hp)),
            scratch_shapes=[pltpu.VMEM((2, TB, TB), F32)] * 2 if kind == "mla" else []),
        out_shape=jax.ShapeDtypeStruct((t, nh * dv), BF16),
        compiler_params=_cparams("arbitrary", "arbitrary"), name="attn_" + kind,
    )(*args)
    has_sink = kind == "swa"
    ckern = functools.partial(_attn_ctx_kernel, dk=dk, dv=dv, has_sink=has_sink)
    cargs = ([sink] if has_sink else []) + [qt, k, vt, o]
    return pl.pallas_call(
        ckern,
        grid_spec=pltpu.PrefetchScalarGridSpec(
            num_scalar_prefetch=nsp, grid=(nh // 2,),
            in_specs=[pl.BlockSpec((2 * dk, TB), lambda hp, *_: (hp, ntl)),
                      pl.BlockSpec((TB, 2 * dk), lambda hp, *_: (ntl, hp)),
                      pl.BlockSpec((2, 1, dva, TB), lambda hp, *_: (hp, ntl, 0, 0)),
                      pl.BlockSpec(memory_space=pl.ANY)],
            out_specs=pl.BlockSpec((TB, 2 * dv), lambda hp, *_: (ntl, hp))),
        out_shape=jax.ShapeDtypeStruct((t, nh * dv), BF16),
        input_output_aliases={len(cargs) - 1: 0},
        compiler_params=_cparams("arbitrary"), name="attn_ctx_" + kind,
    )(*cargs)


def _router_kernel(x_ref, g_ref, sh_ref, sc_ref, rt_ref, h_ref, aff_ref):
    h = _normmod(x_ref[...], g_ref[...], sh_ref[0, 0], sc_ref[0, 0])
    h_ref[...] = h.astype(BF16)
    lg = lax.dot_general(rt_ref[...], h, (((1,), (1,)), ((), ())),
                         precision=HIGHEST, preferred_element_type=F32)
    lg = lg - jnp.max(lg, axis=0, keepdims=True)
    e = jnp.exp(lg)
    aff_ref[...] = e / jnp.sum(e, axis=0, keepdims=True)


def _router(xa, g, mods, router_t, ntl):
    t, d = xa.shape
    ne = router_t.shape[0]

    def modspec(k):
        return pl.BlockSpec((1, 1, 1, d), lambda i: (k, i // ntl, 0, 0))

    return pl.pallas_call(
        _router_kernel, grid=(t // TB,),
        in_specs=[pl.BlockSpec((TB, d), lambda i: (i, 0)),
                  pl.BlockSpec((1, d), lambda i: (0, 0)),
                  modspec(3), modspec(4),
                  pl.BlockSpec((ne, d), lambda i: (0, 0))],
        out_specs=[pl.BlockSpec((TB, d), lambda i: (i, 0)),
                   pl.BlockSpec((ne, TB), lambda i: (0, i))],
        out_shape=[jax.ShapeDtypeStruct((t, d), BF16), jax.ShapeDtypeStruct((ne, t), F32)],
        compiler_params=_cparams("arbitrary"), name="router",
    )(xa, g.reshape(1, d), mods, mods, router_t)


def _topk_kernel(aff_ref, pos_ref, gate_ref, off_ref, *, cap, base):
    x = aff_ref[...]
    ne, nc, _ = x.shape
    bits = lax.bitcast_convert_type(x, jnp.int32)

    def count(mask):
        c = jnp.sum(jnp.where(mask, 1.0, 0.0), axis=1, keepdims=True)
        return jnp.sum(c, axis=2, keepdims=True)

    thr = jnp.zeros((ne, 1, 1), jnp.int32)
    for b in range(30, -1, -1):
        cand = thr | (1 << b)
        thr = jnp.where(count(bits >= cand) >= cap, cand, thr)

    ia = lax.broadcasted_iota(jnp.int32, (LANES, LANES), 0)
    ib = lax.broadcasted_iota(jnp.int32, (LANES, LANES), 1)
    upper = jnp.where(ia <= ib, 1.0, 0.0)
    ones = jnp.ones((LANES, LANES), F32)
    ca = lax.broadcasted_iota(jnp.int32, (nc, nc), 0)
    cb = lax.broadcasted_iota(jnp.int32, (nc, nc), 1)
    lower = jnp.where(cb < ca, 1.0, 0.0)

    def prefix(mask):
        m2 = jnp.where(mask, 1.0, 0.0).reshape(ne * nc, LANES)
        within = jnp.dot(m2, upper, precision=HIGHEST, preferred_element_type=F32)
        tot = jnp.dot(m2, ones, precision=HIGHEST, preferred_element_type=F32).reshape(ne, nc, LANES)
        offs = [jnp.dot(lower, tot[e], precision=HIGHEST, preferred_element_type=F32)[None]
                for e in range(ne)]
        off = jnp.concatenate(offs, axis=0)
        return within.reshape(ne, nc, LANES) + off, off

    gt = bits > thr
    eq = bits == thr
    need = cap - count(gt)
    eq_rank, _ = prefix(eq)
    sel = gt | (eq & (eq_rank <= need))
    sel_rank, off = prefix(sel)
    pos_ref[...] = jnp.where(sel, sel_rank - 1.0 + base, -1.0)
    gate_ref[...] = jnp.where(sel, x, 0.0)
    off_ref[...] = off.astype(jnp.int32) + base


def _topk(aff3, cap, base):
    ne, nc, _ = aff3.shape
    kern = functools.partial(_topk_kernel, cap=cap, base=base)
    spec = pl.BlockSpec((ne, nc, LANES), lambda i: (0, 0, 0))
    return pl.pallas_call(
        kern, grid=(1,), in_specs=[spec], out_specs=[spec, spec, spec],
        out_shape=[jax.ShapeDtypeStruct(aff3.shape, F32), jax.ShapeDtypeStruct(aff3.shape, F32),
                   jax.ShapeDtypeStruct(aff3.shape, jnp.int32)],
        compiler_params=_cparams("arbitrary"), name="topk",
    )(aff3)


def _ffn_kernel(offs_ref, pos_ref, gate_ref, h_ref, wg_ref, wu_ref, wd_ref, y_ref,
                xs_ref, xb_ref, gs_ref, *, ng, gt, nf, nrows):
    e = pl.program_id(0)
    j = pl.program_id(1)

    @pl.when(j == 0)
    def _():
        xs_ref[...] = jnp.zeros_like(xs_ref)
        gs_ref[...] = jnp.zeros_like(gs_ref)

    @pl.when(j < ng)
    def _():
        for s in range(gt):
            off = offs_ref[e, j * gt + s]
            w0 = pl.multiple_of((off // 8) * 8, 8)
            prow = pos_ref[0, :, s * TB:(s + 1) * TB]
            grow = gate_ref[0, :, s * TB:(s + 1) * TB]
            ids = (w0 + lax.broadcasted_iota(jnp.int32, (GATHER_WIN, 1), 0)).astype(F32)
            hit = prow == ids
            onehot = jnp.where(hit, 1.0, 0.0).astype(BF16)
            xs_ref[pl.ds(w0, GATHER_WIN), :] += jnp.dot(
                onehot, h_ref[s * TB:(s + 1) * TB, :], preferred_element_type=F32)
            gsel = jnp.sum(jnp.where(hit, grow, 0.0), axis=1, keepdims=True)
            gs_ref[pl.ds(w0, GATHER_WIN), :] += jnp.broadcast_to(gsel, (GATHER_WIN, LANES))

    @pl.when(j == ng)
    def _():
        xb_ref[...] = xs_ref[0:nrows, :].astype(BF16)
        xs_ref[...] = jnp.zeros_like(xs_ref)

    @pl.when(j >= ng)
    def _():
        wg = wg_ref[0, 0].astype(BF16)
        wu = wu_ref[0, 0].astype(BF16)
        wd = wd_ref[0, 0].astype(BF16)
        for b in range(nrows // FFN_ROWS):
            rows = slice(b * FFN_ROWS, (b + 1) * FFN_ROWS)
            x = xb_ref[rows, :]
            a = jnp.dot(x, wg, preferred_element_type=F32)
            u = jnp.dot(x, wu, preferred_element_type=F32)
            hmid = (a * jax.nn.sigmoid(a) * u).astype(BF16)
            xs_ref[rows, :] += jnp.dot(hmid, wd, preferred_element_type=F32)

    @pl.when(j == ng + nf - 1)
    def _():
        y_ref[0, 0:nrows, :] = (xs_ref[0:nrows, :] * gs_ref[0:nrows, 0:1]).astype(y_ref.dtype)
        if y_ref.shape[1] > nrows:
            y_ref[0, nrows:, :] = jnp.zeros((y_ref.shape[1] - nrows, y_ref.shape[2]), y_ref.dtype)


def _ffn(offs, pos, gate, h, wg, wu, wd, *, layer, nslots):
    t, d = h.shape
    nt = t // TB
    _, ne, _, f = wg.shape
    gt = max(g for g in range(1, 9) if nt % g == 0)
    ng = nt // gt
    nf = f // FFN_FCHUNK
    nrows = -(-nslots // FFN_ROWS) * FFN_ROWS
    srows = -(-(nslots + TB) // 16) * 16
    srows = max(srows, nrows)
    xs_rows = max(nrows, nslots + GATHER_WIN)
    kern = functools.partial(_ffn_kernel, ng=ng, gt=gt, nf=nf, nrows=nrows)

    def fchunk(j):
        return jnp.clip(j - ng, 0, nf - 1)

    grid_spec = pltpu.PrefetchScalarGridSpec(
        num_scalar_prefetch=1, grid=(ne, ng + nf),
        in_specs=[
            pl.BlockSpec((1, 1, gt * TB), lambda e, j, o: (e, 0, jnp.minimum(j, ng - 1))),
            pl.BlockSpec((1, 1, gt * TB), lambda e, j, o: (e, 0, jnp.minimum(j, ng - 1))),
            pl.BlockSpec((gt * TB, d), lambda e, j, o: (jnp.minimum(j, ng - 1), 0)),
            pl.BlockSpec((1, 1, d, FFN_FCHUNK), lambda e, j, o: (layer, e, 0, fchunk(j))),
            pl.BlockSpec((1, 1, d, FFN_FCHUNK), lambda e, j, o: (layer, e, 0, fchunk(j))),
            pl.BlockSpec((1, 1, FFN_FCHUNK, d), lambda e, j, o: (layer, e, fchunk(j), 0)),
        ],
        out_specs=pl.BlockSpec((1, srows, d), lambda e, j, o: (e, 0, 0)),
        scratch_shapes=[pltpu.VMEM((xs_rows, d), F32), pltpu.VMEM((nrows, d), BF16),
                        pltpu.VMEM((xs_rows, LANES), F32)])
    return pl.pallas_call(
        kern, grid_spec=grid_spec,
        out_shape=jax.ShapeDtypeStruct((ne, srows, d), BF16),
        compiler_params=_cparams("arbitrary", "arbitrary"), name="ffn",
    )(offs, pos.reshape(ne, 1, t), gate.reshape(ne, 1, t), h, wg, wu, wd)


def _combine_kernel(*refs, ne, srows):
    offs_ref, cnt_ref = refs[0], refs[1]
    posc_ref, xa_ref, gate_ref = refs[2], refs[3], refs[4]
    win_refs = refs[5:5 + ne]
    win2_refs = refs[5 + ne:5 + 2 * ne]
    o_ref = refs[5 + 2 * ne]
    j = pl.program_id(0)
    lane = lax.broadcasted_iota(jnp.int32, (1, TB), 1)
    acc = None
    for e in range(ne):
        w0 = (offs_ref[e, j] // 16) * 16
        pcol = posc_ref[:, e:e + 1]
        onehot = jnp.where(pcol == (w0 + lane).astype(F32), 1.0, 0.0).astype(BF16)
        part = jnp.dot(onehot, win_refs[e][...], preferred_element_type=F32)
        acc = part if acc is None else acc + part
    o_ref[...] = xa_ref[...] + gate_ref[0, 0] * acc

    for e in range(ne):
        off = offs_ref[e, j]
        w0 = (off // 16) * 16

        @pl.when(off - w0 + cnt_ref[e, j] > TB)
        def _(e=e, w0=w0):
            w1 = jnp.minimum(w0 + TB, srows - TB)
            ids = w1 + lane
            hit = (posc_ref[:, e:e + 1] == ids.astype(F32)) & (ids >= w0 + TB)
            oh = jnp.where(hit, 1.0, 0.0).astype(BF16)
            o_ref[...] += gate_ref[0, 0] * jnp.dot(oh, win2_refs[e][...], preferred_element_type=F32)


def _combine(offs, cnts, pos_cols, xa, mods, ys, ntl):
    t, d = xa.shape
    nt = t // TB
    ne, srows, _ = ys.shape
    assert srows % 16 == 0
    ys2 = ys.reshape(ne * srows, d)

    def win_spec(e):
        def imap(j, offs, cnts):
            return ((e * srows // 16 + offs[e, j] // 16) * 16, 0)
        return pl.BlockSpec((pl.Element(TB), pl.Element(d)), imap)

    def win2_spec(e):
        def imap(j, offs, cnts):
            off = offs[e, j]
            w0 = (off // 16) * 16
            need = off - w0 + cnts[e, j] > TB
            w1 = jnp.where(need, jnp.minimum(w0 + TB, srows - TB), 0)
            return ((e * srows // 16 + w1 // 16) * 16, 0)
        return pl.BlockSpec((pl.Element(TB), pl.Element(d)), imap)

    in_specs = [
        pl.BlockSpec((TB, ne), lambda j, o, c: (j, 0)),
        pl.BlockSpec((TB, d), lambda j, o, c: (j, 0)),
        pl.BlockSpec((1, 1, 1, d), lambda j, o, c: (5, j // ntl, 0, 0)),
    ] + [win_spec(e) for e in range(ne)] + [win2_spec(e) for e in range(ne)]
    grid_spec = pltpu.PrefetchScalarGridSpec(
        num_scalar_prefetch=2, grid=(nt,), in_specs=in_specs,
        out_specs=pl.BlockSpec((TB, d), lambda j, o, c: (j, 0)))
    kern = functools.partial(_combine_kernel, ne=ne, srows=srows)
    return pl.pallas_call(
        kern, grid_spec=grid_spec, out_shape=jax.ShapeDtypeStruct((t, d), F32),
        compiler_params=_cparams("arbitrary"), name="combine",
    )(offs, cnts, pos_cols, xa, mods, *([ys2] * (2 * ne)))


def _final_kernel(x_ref, g_ref, o_ref):
    x = x_ref[...]
    o_ref[...] = x * lax.rsqrt(jnp.mean(x * x, axis=-1, keepdims=True) + RMS_EPS) * g_ref[...]


def _final_norm(xa, g, seq):
    d = xa.shape[1]
    return pl.pallas_call(
        _final_kernel, grid=(seq // TB,),
        in_specs=[pl.BlockSpec((TB, d), lambda i: (i, 0)), pl.BlockSpec((1, d), lambda i: (0, 0))],
        out_specs=pl.BlockSpec((TB, d), lambda i: (i, 0)),
        out_shape=jax.ShapeDtypeStruct((seq, d), F32),
        compiler_params=_cparams("arbitrary"), name="final_norm",
    )(xa, g.reshape(1, d))


def _rope_tables(seq, t, rot_dim, lane_layout):
    quarter = rot_dim // 4
    rows = seq // GRID_W
    inv_freq = ROPE_THETA ** (-jnp.arange(quarter, dtype=F32) / quarter)
    ang_r = jnp.arange(rows, dtype=F32)[:, None] * inv_freq
    ang_c = jnp.arange(GRID_W, dtype=F32)[:, None] * inv_freq

    def table(fn, fill):
        r = jnp.broadcast_to(fn(ang_r)[:, None, :], (rows, GRID_W, quarter))
        c = jnp.broadcast_to(fn(ang_c)[None, :, :], (rows, GRID_W, quarter))
        blk = jnp.concatenate([r, r, c, c], axis=2)
        parts, prev = [], 0
        for lo, hi in lane_layout:
            if lo > prev:
                parts.append(jnp.full((rows, GRID_W, lo - prev), fill, F32))
            parts.append(blk)
            prev = hi
        if prev < LANES:
            parts.append(jnp.full((rows, GRID_W, LANES - prev), fill, F32))
        lat = jnp.concatenate(parts, axis=2).reshape(seq, LANES)
        return jnp.concatenate([lat, jnp.full((t - seq, LANES), fill, F32)], axis=0)

    return table(jnp.cos, 1.0), table(jnp.sin, 0.0)


def _rope_partner_perm(rot_dim):
    q = rot_dim // 4
    src = np.concatenate([np.arange(q, 2 * q), np.arange(0, q), np.arange(3 * q, 4 * q), np.arange(2 * q, 3 * q)])
    sign = np.concatenate([-np.ones(q), np.ones(q), -np.ones(q), np.ones(q)]).astype(np.float32)
    return src, sign


def _swa_mask(ntl):
    out = []
    for i in (0, 1, ntl - 1):
        ts = min(max(i - 1, 0), ntl - 3)
        kpos = ts * TB + np.arange(3 * TB)[:, None]
        qpos = i * TB + np.arange(TB)[None, :]
        out.append(np.where(np.abs(kpos - qpos) <= SWA_WINDOW, 0.0, NEG).astype(np.float32))
    return jnp.asarray(np.stack(out)[:, None])


def _na_bias(rpb, rows, ntl):
    nh = rpb.shape[0]
    rpt = TB // GRID_W
    kr = min(NA_WIN_ROWS, rows)
    c = np.arange(GRID_W)
    cs = np.clip(c - NA_WIN_COLS // 2, 0, GRID_W - NA_WIN_COLS)
    okc = (c[:, None] >= cs[None, :]) & (c[:, None] < cs[None, :] + NA_WIN_COLS)
    dc = np.clip(c[:, None] - c[None, :] + NA_WIN_COLS - 1, 0, 2 * NA_WIN_COLS - 2)
    toep = jnp.where(okc[None, None], rpb[:, :, dc] * LOG2E, NEG)
    neg = jnp.full((nh, GRID_W, GRID_W), NEG, F32)
    out = []
    for i in (0, 1, ntl - 1):
        ts = min(max(i - 1, 0), ntl - 3)
        key_rows = []
        for krow in range(3 * rpt):
            r2 = ts * rpt + krow
            blocks = []
            for qrow in range(rpt):
                r = i * rpt + qrow
                rs = min(max(r - kr // 2, 0), rows - kr)
                blocks.append(toep[:, r2 - r + NA_WIN_ROWS - 1] if rs <= r2 < rs + kr else neg)
            key_rows.append(jnp.concatenate(blocks, axis=2))
        out.append(jnp.concatenate(key_rows, axis=1))
    return jnp.stack(out, axis=0).astype(F32)


def _mixer_na(xa, g, mods, w_qkv, w_o, rpb, ntl, seq):
    d = xa.shape[1]
    hd = d // NA_HEADS
    n = NA_HEADS * hd
    scale = hd ** -0.5 * LOG2E
    colscale = jnp.concatenate([jnp.full((n,), scale, F32), jnp.ones((2 * n,), F32)])
    q, k, v = _proj(xa, w_qkv.astype(BF16), ntl=ntl, mods=mods, norm_g=g, mod_idx=(0, 1),
                    colscale=colscale, splits=[(0, n), (n, 2 * n), (2 * n, 3 * n)], name="na_qkv")
    bias = _na_bias(rpb, seq // GRID_W, ntl)
    o = _attn("na", q, k, v, dk=hd, dv=hd, ntl=ntl, bias=bias)
    (xa,) = _proj(o, w_o.astype(BF16), ntl=ntl, mods=mods, resid=xa, gate_idx=2,
                  out_dtypes=[F32], name="na_out")
    return xa


def _mixer_swa(xa, g, mods, w_qkv, w_o, sink, ntl, seq):
    t, d = xa.shape
    hd = d // SWA_Q_HEADS
    nq, nkv = SWA_Q_HEADS * hd, SWA_KV_HEADS * hd
    group = SWA_Q_HEADS // SWA_KV_HEADS
    scale = hd ** -0.5 * LOG2E
    wq = w_qkv[:, :nq]
    wk = jnp.repeat(w_qkv[:, nq:nq + nkv].reshape(d, SWA_KV_HEADS, hd), group, axis=1).reshape(d, nq)
    wv = jnp.repeat(w_qkv[:, nq + nkv:].reshape(d, SWA_KV_HEADS, hd), group, axis=1).reshape(d, nq)
    w = jnp.concatenate([wq, wk, wv], axis=1)
    src, sign = _rope_partner_perm(hd)
    nblk = 2 * nq // hd
    src_full = (np.arange(nblk)[:, None] * hd + src[None, :]).reshape(-1)
    sign_full = np.tile(sign, nblk)
    w2 = w[:, src_full] * sign_full[None, :]
    cos, sin = _rope_tables(seq, t, hd, [(0, hd), (hd, 2 * hd)])
    colscale = jnp.concatenate([jnp.full((nq,), scale, F32), jnp.ones((2 * nq,), F32)])
    q, k, v = _proj(xa, w.astype(BF16), ntl=ntl, mods=mods, norm_g=g, mod_idx=(0, 1),
                    w2=w2.astype(BF16), cos=cos, sin=sin, colscale=colscale,
                    splits=[(0, nq), (nq, 2 * nq), (2 * nq, 3 * nq)], name="swa_qkv")
    o = _attn("swa", q, k, v, dk=hd, dv=hd, ntl=ntl, bias=_swa_mask(ntl), sink=sink.astype(F32) * LOG2E)
    (xa,) = _proj(o, w_o.astype(BF16), ntl=ntl, mods=mods, resid=xa, gate_idx=2,
                  out_dtypes=[F32], name="swa_out")
    return xa


def _mixer_mla(xa, g, mods, w_dq, q_norm, w_uq, w_dkv, kv_norm, w_ukv, w_o, ntl, seq):
    t, d = xa.shape
    nh = MLA_HEADS
    qr = w_dq.shape[1]
    kvr = kv_norm.shape[0]
    nope, rope, dv = MLA_NOPE_DIM, MLA_ROPE_DIM, MLA_V_DIM
    qk = nope + rope
    pad = LANES - qk
    scale = qk ** -0.5 * LOG2E
    kin = kvr + LANES
    w_dn = jnp.concatenate([w_dq, w_dkv, jnp.zeros((d, LANES - rope), F32)], axis=1)
    g_dn = jnp.concatenate([q_norm, kv_norm, jnp.ones((LANES,), F32)])
    cqn, kvin = _proj(xa, w_dn.astype(BF16), ntl=ntl, mods=mods, norm_g=g, mod_idx=(0, 1),
                      rms_segs=[(0, qr), (qr, qr + kvr)], rms_g=g_dn,
                      splits=[(0, qr), (qr, qr + kin)], name="mla_down")
    src, sign = _rope_partner_perm(rope)
    src_head = np.concatenate([np.arange(nope), nope + src, np.arange(qk, LANES)])
    sign_head = np.concatenate([np.zeros(nope, np.float32), sign, np.zeros(pad, np.float32)])
    src_full = (np.arange(nh)[:, None] * LANES + src_head[None, :]).reshape(-1)
    sign_full = np.tile(sign_head, nh)
    cos, sin = _rope_tables(seq, t, rope, [(nope, qk)])
    wq = jnp.pad(w_uq.reshape(qr, nh, qk), ((0, 0), (0, 0), (0, pad))).reshape(qr, nh * LANES)
    wq2 = wq[:, src_full] * sign_full[None, :]
    (q,) = _proj(cqn, wq.astype(BF16), ntl=ntl, w2=wq2.astype(BF16), cos=cos, sin=sin,
                 colscale=jnp.full((nh * LANES,), scale, F32), name="mla_q")
    w_ukv3 = w_ukv.reshape(kvr, nh, nope + dv)
    wk_top = jnp.pad(w_ukv3[:, :, :nope], ((0, 0), (0, 0), (0, LANES - nope)))
    eye = jnp.eye(rope, dtype=F32)
    wk_rope = jnp.pad(eye, ((0, LANES - rope), (nope, pad)))
    wk_bot = jnp.broadcast_to(wk_rope[:, None, :], (LANES, nh, LANES))
    wk = jnp.concatenate([wk_top, wk_bot], axis=0).reshape(kin, nh * LANES)
    wk2 = wk[:, src_full] * sign_full[None, :]
    wv = jnp.concatenate([w_ukv3[:, :, nope:].reshape(kvr, nh * dv), jnp.zeros((LANES, nh * dv), F32)], axis=0)
    (k,) = _proj(kvin, wk.astype(BF16), ntl=ntl, w2=wk2.astype(BF16), cos=cos, sin=sin, name="mla_k")
    (v,) = _proj(kvin, wv.astype(BF16), ntl=ntl, name="mla_v")
    o = _attn("mla", q, k, v, dk=LANES, dv=dv, ntl=ntl)
    (xa,) = _proj(o, w_o.astype(BF16), ntl=ntl, mods=mods, resid=xa, gate_idx=2,
                  out_dtypes=[F32], name="mla_out")
    return xa


def _moe(xa, g, mods, router, w_gate, w_up, w_down, layer, ntl, seq, with_ctx):
    t, d = xa.shape
    ne = router.shape[1]
    nctx = t - seq
    h, aff = _router(xa, g, mods, router.T, ntl)
    cap_l = EC_CAPACITY_FACTOR * seq // ne
    pos_l, gate_l, off_l = _topk(aff[:, :seq].reshape(ne, seq // CHUNK, CHUNK), cap_l, 0)
    per_tile = TB // CHUNK
    pos = [pos_l.reshape(ne, seq)]
    gate = [gate_l.reshape(ne, seq)]
    offs = [off_l[:, ::per_tile, 0]]
    if with_ctx:
        cap_c = EC_CAPACITY_FACTOR * nctx // ne
        cpad = 8 * CHUNK
        aff_c = jnp.concatenate([aff[:, seq:], jnp.full((ne, cpad - nctx), -1.0, F32)], axis=1)
        pos_c, gate_c, off_c = _topk(aff_c.reshape(ne, 8, CHUNK), cap_c, cap_l)
        pos.append(pos_c.reshape(ne, cpad)[:, :nctx])
        gate.append(gate_c.reshape(ne, cpad)[:, :nctx])
        offs.append(off_c[:, 0:nctx // CHUNK:per_tile, 0])
        nslots = cap_l + cap_c
    else:
        pos.append(jnp.full((ne, nctx), -1.0, F32))
        gate.append(jnp.zeros((ne, nctx), F32))
        offs.append(jnp.full((ne, nctx // TB), cap_l, jnp.int32))
        nslots = cap_l
    pos = jnp.concatenate(pos, axis=1)
    gate = jnp.concatenate(gate, axis=1)
    offs = jnp.concatenate(offs + [jnp.full((ne, 1), nslots, jnp.int32)], axis=1)
    cnts = offs[:, 1:] - offs[:, :-1]
    ys = _ffn(offs, pos, gate, h, w_gate, w_up, w_down, layer=layer, nslots=nslots)
    return _combine(offs, cnts, pos.T, xa, mods, ys, ntl)


def kernel(x, c, ctx, c_ctx, ada_w, ada_b, norm_mix, norm_ffn, na_w_qkv, na_w_o, na_rpb, swa_w_qkv, swa_w_o, swa_sink, mla_w_dq, mla_q_norm, mla_w_uq, mla_w_dkv, mla_kv_norm, mla_w_ukv, mla_w_o, moe_router, moe_w_gate, moe_w_up, moe_w_down, final_norm):
    assert x.shape[0] == 1 and c.shape[0] == 1 and ctx.shape[0] == 1
    seq, d = x.shape[1], x.shape[2]
    nctx = ctx.shape[1]
    assert seq % TB == 0 and nctx == TB and seq // TB >= 4
    depth = ada_w.shape[0]
    ntl = seq // TB
    xa = jnp.concatenate([x[0], ctx[0]], axis=0)
    cs = jnp.concatenate([c, c_ctx[None, :], jnp.zeros((6, d), F32)], axis=0)
    ada = _ada(cs, ada_w, ada_b)
    mods_all = jnp.transpose(ada[:, :2].reshape(depth, 2, 6, d), (0, 2, 1, 3))[:, :, :, None, :]
    for i in range(depth):
        mods = mods_all[i]
        kind, slot = i % N_MIXERS, i // N_MIXERS
        if kind == 0:
            xa = _mixer_na(xa, norm_mix[i], mods, na_w_qkv[slot], na_w_o[slot], na_rpb[slot], ntl, seq)
        elif kind == 1:
            xa = _mixer_swa(xa, norm_mix[i], mods, swa_w_qkv[slot], swa_w_o[slot], swa_sink[slot], ntl, seq)
        else:
            xa = _mixer_mla(xa, norm_mix[i], mods, mla_w_dq[slot], mla_q_norm[slot], mla_w_uq[slot],
                            mla_w_dkv[slot], mla_kv_norm[slot], mla_w_ukv[slot], mla_w_o[slot], ntl, seq)
        xa = _moe(xa, norm_ffn[i], mods, moe_router[i], moe_w_gate, moe_w_up, moe_w_down, i,
                  ntl, seq, with_ctx=i < depth - 1)
    return _final_norm(xa, final_norm, seq)[None]
```

```python
import functools
import math

import jax
import jax.numpy as jnp
import numpy as np
from jax import lax
from jax.experimental import pallas as pl
from jax.experimental.pallas import tpu as pltpu

F32 = jnp.float32
BF16 = jnp.bfloat16
HIGHEST = lax.Precision.HIGHEST

GRID_W = 64
N_MIXERS = 3
RMS_EPS = 1e-6
ROPE_THETA = 10000.0
NA_HEADS = 16
NA_WIN_ROWS = 8
NA_WIN_COLS = 16
SWA_Q_HEADS = 16
SWA_KV_HEADS = 4
SWA_WINDOW = 128
MLA_HEADS = 16
MLA_NOPE_DIM = 64
MLA_ROPE_DIM = 32
MLA_V_DIM = 64
N_EXPERTS = 16
EC_CAPACITY_FACTOR = 2

TB = 256
CHUNK = 128
LANES = 128
NEG = -1e30
LOG2E = math.log2(math.e)
FFN_ROWS = 416
FFN_FCHUNK = 512
GATHER_WIN = TB + 16
COMBINE_WIN = 64
VMEM_LIMIT = 56 * 1024 * 1024


def _cparams(*sem):
    return pltpu.CompilerParams(dimension_semantics=sem, vmem_limit_bytes=VMEM_LIMIT)


def _ada_kernel(cs_ref, w_ref, b_ref, o_ref):
    x = cs_ref[...]
    x = x * jax.nn.sigmoid(x)
    y = jnp.dot(x, w_ref[0], precision=HIGHEST, preferred_element_type=F32)
    o_ref[0] = y + b_ref[0]


def _ada(cs, ada_w, ada_b):
    depth, d, n = ada_w.shape
    nb = n // d
    return pl.pallas_call(
        _ada_kernel,
        grid=(depth, nb),
        in_specs=[
            pl.BlockSpec((8, d), lambda i, k: (0, 0)),
            pl.BlockSpec((1, d, d), lambda i, k: (i, 0, k)),
            pl.BlockSpec((1, 1, d), lambda i, k: (i, 0, k)),
        ],
        out_specs=pl.BlockSpec((1, 8, d), lambda i, k: (i, 0, k)),
        out_shape=jax.ShapeDtypeStruct((depth, 8, n), F32),
        compiler_params=_cparams("arbitrary", "arbitrary"),
        name="ada",
    )(cs, ada_w, ada_b.reshape(depth, 1, n))


def _normmod(x, g, shift, scale):
    y = x * lax.rsqrt(jnp.mean(x * x, axis=-1, keepdims=True) + RMS_EPS)
    return (y * g) * (1.0 + scale) + shift


def _proj_kernel(*refs, normmod, rms_segs, rope_cols, colscale, resid, splits, out_t):
    it = iter(refs)
    x_ref = next(it)
    if normmod:
        g_ref, sh_ref, sc_ref = next(it), next(it), next(it)
    w_ref = next(it)
    if rms_segs:
        g2_ref = next(it)
    if rope_cols:
        w2_ref, cos_ref, sin_ref = next(it), next(it), next(it)
    if colscale:
        cs_ref = next(it)
    if resid:
        res_ref, gate_ref = next(it), next(it)
    out_refs = list(it)

    x = x_ref[...]
    if normmod:
        x = _normmod(x.astype(F32), g_ref[...], sh_ref[0, 0], sc_ref[0, 0])
    xb = x.astype(BF16)
    y = jnp.dot(xb, w_ref[...], preferred_element_type=F32)
    if rms_segs:
        parts = []
        prev = 0
        for (a, b) in rms_segs:
            if a > prev:
                parts.append(y[:, prev:a])
            seg = y[:, a:b]
            seg = seg * lax.rsqrt(jnp.mean(seg * seg, axis=-1, keepdims=True) + RMS_EPS)
            parts.append(seg * g2_ref[:, a:b])
            prev = b
        if prev < y.shape[1]:
            parts.append(y[:, prev:])
        y = jnp.concatenate(parts, axis=1)
    if rope_cols:
        y2 = jnp.dot(xb, w2_ref[...], preferred_element_type=F32)
        reps = rope_cols // LANES
        cos = jnp.tile(cos_ref[...], (1, reps))
        sin = jnp.tile(sin_ref[...], (1, reps))
        yr = y[:, :rope_cols] * cos + y2 * sin
        y = yr if rope_cols == y.shape[1] else jnp.concatenate([yr, y[:, rope_cols:]], axis=1)
    if colscale:
        y = y * cs_ref[...]
    if resid:
        y = res_ref[...] + gate_ref[0, 0] * y
    for o_ref, (a, b), tr in zip(out_refs, splits, out_t):
        if tr:
            o_ref[0] = y[:, a:b].T.astype(o_ref.dtype)
        else:
            o_ref[...] = y[:, a:b].astype(o_ref.dtype)


def _proj(x, w, *, ntl, mods=None, norm_g=None, mod_idx=None, rms_segs=None, rms_g=None,
          w2=None, cos=None, sin=None, colscale=None, resid=None, gate_idx=None,
          splits=None, out_dtypes=None, out_t=None, name="proj"):
    t, kdim = x.shape
    n = w.shape[1]
    nt = t // TB
    splits = splits or [(0, n)]
    out_dtypes = out_dtypes or [BF16] * len(splits)

    def modspec(k):
        dm = mods.shape[-1]
        return pl.BlockSpec((1, 1, 1, dm), lambda i: (k, i // ntl, 0, 0))

    args, specs = [x], [pl.BlockSpec((TB, kdim), lambda i: (i, 0))]
    if norm_g is not None:
        args += [norm_g.reshape(1, kdim), mods, mods]
        specs += [pl.BlockSpec((1, kdim), lambda i: (0, 0)), modspec(mod_idx[0]), modspec(mod_idx[1])]
    args.append(w)
    specs.append(pl.BlockSpec((kdim, n), lambda i: (0, 0)))
    if rms_segs:
        args.append(rms_g.reshape(1, n))
        specs.append(pl.BlockSpec((1, n), lambda i: (0, 0)))
    rope_cols = 0
    if w2 is not None:
        rope_cols = w2.shape[1]
        args += [w2, cos, sin]
        specs += [pl.BlockSpec((kdim, rope_cols), lambda i: (0, 0)),
                  pl.BlockSpec((TB, LANES), lambda i: (i, 0)),
                  pl.BlockSpec((TB, LANES), lambda i: (i, 0))]
    if colscale is not None:
        args.append(colscale.reshape(1, n))
        specs.append(pl.BlockSpec((1, n), lambda i: (0, 0)))
    if resid is not None:
        args += [resid, mods]
        specs += [pl.BlockSpec((TB, n), lambda i: (i, 0)), modspec(gate_idx)]
    out_t = out_t or [False] * len(splits)
    out_shape = [jax.ShapeDtypeStruct((nt, b - a, TB) if tr else (t, b - a), dt)
                 for (a, b), dt, tr in zip(splits, out_dtypes, out_t)]
    out_specs = [pl.BlockSpec((1, b - a, TB), lambda i: (i, 0, 0)) if tr
                 else pl.BlockSpec((TB, b - a), lambda i: (i, 0)) for (a, b), tr in zip(splits, out_t)]
    kern = functools.partial(
        _proj_kernel, normmod=norm_g is not None, rms_segs=rms_segs, rope_cols=rope_cols,
        colscale=colscale is not None, resid=resid is not None, splits=splits, out_t=out_t)
    outs = pl.pallas_call(
        kern, grid=(nt,), in_specs=specs, out_specs=out_specs, out_shape=out_shape,
        compiler_params=_cparams("arbitrary"), name=name,
    )(*args)
    return outs


ONES_ROWS = 16


def _softmax_step(st, vt, carry):
    m, acc = carry
    m_new = jnp.maximum(m, jnp.max(st, axis=0, keepdims=True))
    alpha = jnp.exp2(m - m_new)
    p = jnp.exp2(st - m_new)
    acc = alpha * acc + jnp.dot(vt, p.astype(BF16), preferred_element_type=F32)
    return m_new, acc


def _softmax_init(dv):
    return (jnp.full((1, TB), NEG, F32), jnp.zeros((dv + ONES_ROWS, TB), F32))


def _softmax_finish(carry, dv, sink=None):
    m, acc = carry
    num, l = acc[:dv], acc[dv:dv + 1]
    if sink is not None:
        m_f = jnp.maximum(m, sink)
        a = jnp.exp2(m - m_f)
        l = l * a + jnp.exp2(sink - m_f)
        num = num * a
    return (num / l).T


def _scores_t(k, qt):
    return jnp.dot(k, qt, preferred_element_type=F32)


def _attn_kernel(*refs, kind, dk, dv, ntl, hps):
    heads = range(hps)
    if kind == "na":
        q_ref, k_ref, vt_ref, bias_ref, o_ref = refs
    elif kind == "swa":
        sink_ref, q_ref, k_ref, vt_ref, bias_ref, o_ref = refs
    else:
        q_ref, k_ref, vt_ref, o_ref, *scratch = refs
    hp = pl.program_id(0)
    i = pl.program_id(1)
    shared_kv = kind == "swa"
    qs = [q_ref[0, hh * dk:(hh + 1) * dk, :] for hh in heads]
    ones = jnp.ones((ONES_ROWS, TB), BF16)

    def scores(hh, kt):
        start = pl.multiple_of(kt * TB, TB)
        if shared_kv:
            return _scores_t(k_ref[0, pl.ds(start, TB), :], qs[hh])
        return _scores_t(k_ref[pl.ds(start, TB), hh * dk:(hh + 1) * dk], qs[hh])

    def vt_tile(hh, kt):
        lo = 0 if shared_kv else hh * dv
        return jnp.concatenate([vt_ref[kt, lo:lo + dv, :], ones], axis=0)

    carry = [_softmax_init(dv) for _ in heads]
    if kind == "mla":
        sa_ref, sb_ref = scratch

        def half_step(kt, cur_ref, nxt_ref, states):
            for hh in heads:
                nxt_ref[hh] = scores(hh, kt + 1)
            return tuple(_softmax_step(cur_ref[hh], vt_tile(hh, kt), states[hh]) for hh in heads)

        def body(j, states):
            states = half_step(2 * j, sa_ref, sb_ref, states)
            return half_step(2 * j + 1, sb_ref, sa_ref, states)

        for hh in heads:
            sa_ref[hh] = scores(hh, 0)
        carry = lax.fori_loop(0, ntl // 2, body, tuple(carry), unroll=4 if ntl % 8 == 0 else 2)
        carry = [_softmax_step(sa_ref[hh], vt_tile(hh, ntl), carry[hh]) for hh in heads]
    else:
        ts = jnp.clip(i - 1, 0, ntl - 3)
        tiles = [ts, ts + 1, ts + 2, ntl]

        def masked_scores(hh, s):
            st = scores(hh, tiles[s])
            if s == 3:
                return st
            return st + bias_ref[0, hh if kind == "na" else 0, s * TB:(s + 1) * TB, :]

        sts = [masked_scores(hh, 0) for hh in heads]
        for s in range(4):
            nxt = [masked_scores(hh, s + 1) for hh in heads] if s < 3 else None
            for hh in heads:
                carry[hh] = _softmax_step(sts[hh], vt_tile(hh, tiles[s]), carry[hh])
            sts = nxt
    outs = []
    for hh in heads:
        sink = sink_ref[hps * hp + hh] if kind == "swa" else None
        outs.append(_softmax_finish(carry[hh], dv, sink))
    o_ref[...] = jnp.concatenate(outs, axis=1).astype(o_ref.dtype)


def _attn_ctx_kernel(*refs, dk, dv, hps, has_sink, shared_kv):
    if has_sink:
        sink_ref, q_ref, k_ref, vt_ref, _, o_ref = refs
    else:
        q_ref, k_ref, vt_ref, _, o_ref = refs
    hp = pl.program_id(0)
    ones = jnp.ones((ONES_ROWS, TB), BF16)
    outs = []
    for hh in range(hps):
        k = k_ref[0] if shared_kv else k_ref[:, hh * dk:(hh + 1) * dk]
        lo = 0 if shared_kv else hh * dv
        st = _scores_t(k, q_ref[0, hh * dk:(hh + 1) * dk, :])
        vt = jnp.concatenate([vt_ref[0, lo:lo + dv, :], ones], axis=0)
        carry = _softmax_step(st, vt, _softmax_init(dv))
        outs.append(_softmax_finish(carry, dv, sink_ref[hps * hp + hh] if has_sink else None))
    o_ref[...] = jnp.concatenate(outs, axis=1).astype(o_ref.dtype)


def _attn(kind, qt, k, vt, *, dk, dv, ntl, bias=None, sink=None):
    nt, _, _ = qt.shape
    t = nt * TB
    nh = qt.shape[1] // dk
    shared_kv = kind == "swa"
    hps = 2 if kind == "mla" else 4
    kvs = 1 if shared_kv else hps
    assert not shared_kv or nh // k.shape[0] == hps
    kern = functools.partial(_attn_kernel, kind=kind, dk=dk, dv=dv, ntl=ntl, hps=hps)
    in_specs = [
        pl.BlockSpec((1, hps * dk, TB), lambda hp, i, *_: (i, hp, 0)),
        pl.BlockSpec((1, t, dk), lambda hp, i, *_: (hp, 0, 0)) if shared_kv
        else pl.BlockSpec((t, hps * dk), lambda hp, i, *_: (0, hp)),
        pl.BlockSpec((nt, kvs * dv, TB), lambda hp, i, *_: (0, hp, 0)),
    ]
    assert ntl % 2 == 0
    args = [qt, k, vt]
    nsp = 0
    if kind in ("na", "swa"):
        per_head = kind == "na"

        def bias_map(hp, i, *_):
            pat = jnp.where(i == 0, 0, jnp.where(i >= ntl - 1, 2, 1))
            return (pat, hp if per_head else 0, 0, 0)
        in_specs.append(pl.BlockSpec((1, hps if per_head else 1, 3 * TB, TB), bias_map))
        args.append(bias)
    if kind == "swa":
        nsp = 1
        args = [sink] + args
    o = pl.pallas_call(
        kern,
        grid_spec=pltpu.PrefetchScalarGridSpec(
            num_scalar_prefetch=nsp, grid=(nh // hps, ntl), in_specs=in_specs,
            out_specs=pl.BlockSpec((TB, hps * dv), lambda hp, i, *_: (i, hp)),
            scratch_shapes=[pltpu.VMEM((2, TB, TB), F32)] * 2 if kind == "mla" else []),
        out_shape=jax.ShapeDtypeStruct((t, nh * dv), BF16),
        compiler_params=_cparams("arbitrary", "arbitrary"), name="attn_" + kind,
    )(*args)
    has_sink = kind == "swa"
    ckern = functools.partial(_attn_ctx_kernel, dk=dk, dv=dv, hps=hps, has_sink=has_sink,
                              shared_kv=shared_kv)
    cargs = ([sink] if has_sink else []) + [qt, k, vt, o]
    return pl.pallas_call(
        ckern,
        grid_spec=pltpu.PrefetchScalarGridSpec(
            num_scalar_prefetch=nsp, grid=(nh // hps,),
            in_specs=[pl.BlockSpec((1, hps * dk, TB), lambda hp, *_: (ntl, hp, 0)),
                      pl.BlockSpec((1, TB, dk), lambda hp, *_: (hp, ntl, 0)) if shared_kv
                      else pl.BlockSpec((TB, hps * dk), lambda hp, *_: (ntl, hp)),
                      pl.BlockSpec((1, kvs * dv, TB), lambda hp, *_: (ntl, hp, 0)),
                      pl.BlockSpec(memory_space=pl.ANY)],
            out_specs=pl.BlockSpec((TB, hps * dv), lambda hp, *_: (ntl, hp))),
        out_shape=jax.ShapeDtypeStruct((t, nh * dv), BF16),
        input_output_aliases={len(cargs) - 1: 0},
        compiler_params=_cparams("arbitrary"), name="attn_ctx_" + kind,
    )(*cargs)


def _router_kernel(x_ref, g_ref, sh_ref, sc_ref, rt_ref, h_ref, aff_ref):
    h = _normmod(x_ref[...], g_ref[...], sh_ref[0, 0], sc_ref[0, 0])
    h_ref[...] = h.astype(BF16)
    lg = lax.dot_general(rt_ref[...], h, (((1,), (1,)), ((), ())),
                         precision=HIGHEST, preferred_element_type=F32)
    lg = lg - jnp.max(lg, axis=0, keepdims=True)
    e = jnp.exp(lg)
    aff_ref[...] = e / jnp.sum(e, axis=0, keepdims=True)


def _router(xa, g, mods, router_t, ntl):
    t, d = xa.shape
    ne = router_t.shape[0]

    def modspec(k):
        return pl.BlockSpec((1, 1, 1, d), lambda i: (k, i // ntl, 0, 0))

    return pl.pallas_call(
        _router_kernel, grid=(t // TB,),
        in_specs=[pl.BlockSpec((TB, d), lambda i: (i, 0)),
                  pl.BlockSpec((1, d), lambda i: (0, 0)),
                  modspec(3), modspec(4),
                  pl.BlockSpec((ne, d), lambda i: (0, 0))],
        out_specs=[pl.BlockSpec((TB, d), lambda i: (i, 0)),
                   pl.BlockSpec((ne, TB), lambda i: (0, i))],
        out_shape=[jax.ShapeDtypeStruct((t, d), BF16), jax.ShapeDtypeStruct((ne, t), F32)],
        compiler_params=_cparams("arbitrary"), name="router",
    )(xa, g.reshape(1, d), mods, mods, router_t)


def _topk_kernel(aff_ref, pos_ref, gate_ref, off_ref, *, cap, base):
    x = aff_ref[...]
    ne, nc, _ = x.shape
    bits = lax.bitcast_convert_type(x, jnp.int32)

    def count(mask):
        c = jnp.sum(jnp.where(mask, 1.0, 0.0), axis=1, keepdims=True)
        return jnp.sum(c, axis=2, keepdims=True)

    thr = jnp.zeros((ne, 1, 1), jnp.int32)
    for b in range(30, -1, -1):
        cand = thr | (1 << b)
        thr = jnp.where(count(bits >= cand) >= cap, cand, thr)

    ia = lax.broadcasted_iota(jnp.int32, (LANES, LANES), 0)
    ib = lax.broadcasted_iota(jnp.int32, (LANES, LANES), 1)
    upper = jnp.where(ia <= ib, 1.0, 0.0)
    ones = jnp.ones((LANES, LANES), F32)
    ca = lax.broadcasted_iota(jnp.int32, (nc, nc), 0)
    cb = lax.broadcasted_iota(jnp.int32, (nc, nc), 1)
    lower = jnp.where(cb < ca, 1.0, 0.0)

    def prefix(mask):
        m2 = jnp.where(mask, 1.0, 0.0).reshape(ne * nc, LANES)
        within = jnp.dot(m2, upper, precision=HIGHEST, preferred_element_type=F32)
        tot = jnp.dot(m2, ones, precision=HIGHEST, preferred_element_type=F32).reshape(ne, nc, LANES)
        offs = [jnp.dot(lower, tot[e], precision=HIGHEST, preferred_element_type=F32)[None]
                for e in range(ne)]
        off = jnp.concatenate(offs, axis=0)
        return within.reshape(ne, nc, LANES) + off, off

    gt = bits > thr
    eq = bits == thr
    need = cap - count(gt)
    eq_rank, _ = prefix(eq)
    sel = gt | (eq & (eq_rank <= need))
    sel_rank, off = prefix(sel)
    pos_ref[...] = jnp.where(sel, sel_rank - 1.0 + base, -1.0)
    gate_ref[...] = jnp.where(sel, x, 0.0)
    off_ref[...] = off.astype(jnp.int32) + base


def _topk(aff3, cap, base):
    ne, nc, _ = aff3.shape
    kern = functools.partial(_topk_kernel, cap=cap, base=base)
    spec = pl.BlockSpec((ne, nc, LANES), lambda i: (0, 0, 0))
    return pl.pallas_call(
        kern, grid=(1,), in_specs=[spec], out_specs=[spec, spec, spec],
        out_shape=[jax.ShapeDtypeStruct(aff3.shape, F32), jax.ShapeDtypeStruct(aff3.shape, F32),
                   jax.ShapeDtypeStruct(aff3.shape, jnp.int32)],
        compiler_params=_cparams("arbitrary"), name="topk",
    )(aff3)


def _ffn_kernel(offs_ref, pos_ref, gate_ref, h_ref, wg_ref, wu_ref, wd_ref, y_ref,
                xs_ref, xb_ref, gs_ref, *, ng, gt, nf, nrows):
    e = pl.program_id(0)
    j = pl.program_id(1)

    @pl.when(j == 0)
    def _():
        xs_ref[...] = jnp.zeros_like(xs_ref)
        gs_ref[...] = jnp.zeros_like(gs_ref)

    @pl.when(j < ng)
    def _():
        for s in range(gt):
            off = offs_ref[e, j * gt + s]
            w0 = pl.multiple_of((off // 8) * 8, 8)
            prow = pos_ref[0, :, s * TB:(s + 1) * TB]
            grow = gate_ref[0, :, s * TB:(s + 1) * TB]
            ids = (w0 + lax.broadcasted_iota(jnp.int32, (GATHER_WIN, 1), 0)).astype(F32)
            hit = prow == ids
            onehot = jnp.where(hit, 1.0, 0.0).astype(BF16)
            xs_ref[pl.ds(w0, GATHER_WIN), :] += jnp.dot(
                onehot, h_ref[s * TB:(s + 1) * TB, :], preferred_element_type=F32)
            gsel = jnp.sum(jnp.where(hit, grow, 0.0), axis=1, keepdims=True)
            gs_ref[pl.ds(w0, GATHER_WIN), :] += jnp.broadcast_to(gsel, (GATHER_WIN, LANES))

    @pl.when(j == ng)
    def _():
        xb_ref[...] = xs_ref[0:nrows, :].astype(BF16)
        xs_ref[...] = jnp.zeros_like(xs_ref)

    @pl.when(j >= ng)
    def _():
        wg = wg_ref[0, 0].astype(BF16)
        wu = wu_ref[0, 0].astype(BF16)
        wd = wd_ref[0, 0].astype(BF16)
        for b in range(nrows // FFN_ROWS):
            rows = slice(b * FFN_ROWS, (b + 1) * FFN_ROWS)
            x = xb_ref[rows, :]
            a = jnp.dot(x, wg, preferred_element_type=F32)
            u = jnp.dot(x, wu, preferred_element_type=F32)
            hmid = (a * jax.nn.sigmoid(a) * u).astype(BF16)
            xs_ref[rows, :] += jnp.dot(hmid, wd, preferred_element_type=F32)

    @pl.when(j == ng + nf - 1)
    def _():
        y_ref[0, 0:nrows, :] = (xs_ref[0:nrows, :] * gs_ref[0:nrows, 0:1]).astype(y_ref.dtype)
        if y_ref.shape[1] > nrows:
            y_ref[0, nrows:, :] = jnp.zeros((y_ref.shape[1] - nrows, y_ref.shape[2]), y_ref.dtype)


def _ffn(offs, pos, gate, h, wg, wu, wd, *, layer, nslots):
    t, d = h.shape
    nt = t // TB
    _, ne, _, f = wg.shape
    gt = max(g for g in range(1, 9) if nt % g == 0)
    ng = nt // gt
    nf = f // FFN_FCHUNK
    nrows = -(-nslots // FFN_ROWS) * FFN_ROWS
    srows = -(-(nslots + TB) // 16) * 16
    srows = max(srows, nrows)
    xs_rows = max(nrows, nslots + GATHER_WIN)
    kern = functools.partial(_ffn_kernel, ng=ng, gt=gt, nf=nf, nrows=nrows)

    def fchunk(j):
        return jnp.clip(j - ng, 0, nf - 1)

    grid_spec = pltpu.PrefetchScalarGridSpec(
        num_scalar_prefetch=1, grid=(ne, ng + nf),
        in_specs=[
            pl.BlockSpec((1, 1, gt * TB), lambda e, j, o: (e, 0, jnp.minimum(j, ng - 1))),
            pl.BlockSpec((1, 1, gt * TB), lambda e, j, o: (e, 0, jnp.minimum(j, ng - 1))),
            pl.BlockSpec((gt * TB, d), lambda e, j, o: (jnp.minimum(j, ng - 1), 0)),
            pl.BlockSpec((1, 1, d, FFN_FCHUNK), lambda e, j, o: (layer, e, 0, fchunk(j))),
            pl.BlockSpec((1, 1, d, FFN_FCHUNK), lambda e, j, o: (layer, e, 0, fchunk(j))),
            pl.BlockSpec((1, 1, FFN_FCHUNK, d), lambda e, j, o: (layer, e, fchunk(j), 0)),
        ],
        out_specs=pl.BlockSpec((1, srows, d), lambda e, j, o: (e, 0, 0)),
        scratch_shapes=[pltpu.VMEM((xs_rows, d), F32), pltpu.VMEM((nrows, d), BF16),
                        pltpu.VMEM((xs_rows, LANES), F32)])
    return pl.pallas_call(
        kern, grid_spec=grid_spec,
        out_shape=jax.ShapeDtypeStruct((ne, srows, d), BF16),
        compiler_params=_cparams("arbitrary", "arbitrary"), name="ffn",
    )(offs, pos.reshape(ne, 1, t), gate.reshape(ne, 1, t), h, wg, wu, wd)


def _combine_kernel(*refs, ne, srows):
    offs_ref, cnt_ref = refs[0], refs[1]
    posc_ref, xa_ref, gate_ref = refs[2], refs[3], refs[4]
    win_refs = refs[5:5 + ne]
    win2_refs = refs[5 + ne:5 + 2 * ne]
    o_ref = refs[5 + 2 * ne]
    j = pl.program_id(0)
    lane = lax.broadcasted_iota(jnp.int32, (1, LANES), 1)
    first = lane < COMBINE_WIN
    blocks = []
    for e in range(0, ne, 2):
        w0a = (offs_ref[e, j] // 16) * 16
        w0b = (offs_ref[e + 1, j] // 16) * 16
        ids = jnp.where(first, w0a + lane, w0b + lane - COMBINE_WIN).astype(F32)
        pcol = jnp.where(first, posc_ref[:, e:e + 1], posc_ref[:, e + 1:e + 2])
        blocks.append(jnp.where(pcol == ids, 1.0, 0.0).astype(BF16))
    onehot = jnp.concatenate(blocks, axis=1)
    ycat = jnp.concatenate([w[...] for w in win_refs], axis=0)
    acc = jnp.dot(onehot, ycat, preferred_element_type=F32)
    o_ref[...] = xa_ref[...] + gate_ref[0, 0] * acc

    wide = lax.broadcasted_iota(jnp.int32, (1, TB), 1)
    for e in range(ne):
        off = offs_ref[e, j]
        w0 = (off // 16) * 16

        @pl.when(off - w0 + cnt_ref[e, j] > COMBINE_WIN)
        def _(e=e, w0=w0):
            w1 = jnp.minimum(w0 + COMBINE_WIN, srows - TB)
            ids = w1 + wide
            hit = (posc_ref[:, e:e + 1] == ids.astype(F32)) & (ids >= w0 + COMBINE_WIN)
            oh = jnp.where(hit, 1.0, 0.0).astype(BF16)
            o_ref[...] += gate_ref[0, 0] * jnp.dot(oh, win2_refs[e][...], preferred_element_type=F32)


def _combine(offs, cnts, pos_cols, xa, mods, ys, ntl):
    t, d = xa.shape
    nt = t // TB
    ne, srows, _ = ys.shape
    assert srows % 16 == 0
    ys2 = ys.reshape(ne * srows, d)

    assert ne % 2 == 0 and 2 * COMBINE_WIN == LANES

    def win_spec(e):
        def imap(j, offs, cnts):
            return ((e * srows // 16 + offs[e, j] // 16) * 16, 0)
        return pl.BlockSpec((pl.Element(COMBINE_WIN), pl.Element(d)), imap)

    def win2_spec(e):
        def imap(j, offs, cnts):
            off = offs[e, j]
            w0 = (off // 16) * 16
            need = off - w0 + cnts[e, j] > COMBINE_WIN
            w1 = jnp.where(need, jnp.minimum(w0 + COMBINE_WIN, srows - TB), 0)
            return ((e * srows // 16 + w1 // 16) * 16, 0)
        return pl.BlockSpec((pl.Element(TB), pl.Element(d)), imap)

    in_specs = [
        pl.BlockSpec((TB, ne), lambda j, o, c: (j, 0)),
        pl.BlockSpec((TB, d), lambda j, o, c: (j, 0)),
        pl.BlockSpec((1, 1, 1, d), lambda j, o, c: (5, j // ntl, 0, 0)),
    ] + [win_spec(e) for e in range(ne)] + [win2_spec(e) for e in range(ne)]
    grid_spec = pltpu.PrefetchScalarGridSpec(
        num_scalar_prefetch=2, grid=(nt,), in_specs=in_specs,
        out_specs=pl.BlockSpec((TB, d), lambda j, o, c: (j, 0)))
    kern = functools.partial(_combine_kernel, ne=ne, srows=srows)
    return pl.pallas_call(
        kern, grid_spec=grid_spec, out_shape=jax.ShapeDtypeStruct((t, d), F32),
        compiler_params=_cparams("arbitrary"), name="combine",
    )(offs, cnts, pos_cols, xa, mods, *([ys2] * (2 * ne)))


def _final_kernel(x_ref, g_ref, o_ref):
    x = x_ref[...]
    o_ref[...] = x * lax.rsqrt(jnp.mean(x * x, axis=-1, keepdims=True) + RMS_EPS) * g_ref[...]


def _final_norm(xa, g, seq):
    d = xa.shape[1]
    return pl.pallas_call(
        _final_kernel, grid=(seq // TB,),
        in_specs=[pl.BlockSpec((TB, d), lambda i: (i, 0)), pl.BlockSpec((1, d), lambda i: (0, 0))],
        out_specs=pl.BlockSpec((TB, d), lambda i: (i, 0)),
        out_shape=jax.ShapeDtypeStruct((seq, d), F32),
        compiler_params=_cparams("arbitrary"), name="final_norm",
    )(xa, g.reshape(1, d))


def _rope_tables(seq, t, rot_dim, lane_layout):
    quarter = rot_dim // 4
    rows = seq // GRID_W
    inv_freq = ROPE_THETA ** (-jnp.arange(quarter, dtype=F32) / quarter)
    ang_r = jnp.arange(rows, dtype=F32)[:, None] * inv_freq
    ang_c = jnp.arange(GRID_W, dtype=F32)[:, None] * inv_freq

    def table(fn, fill):
        r = jnp.broadcast_to(fn(ang_r)[:, None, :], (rows, GRID_W, quarter))
        c = jnp.broadcast_to(fn(ang_c)[None, :, :], (rows, GRID_W, quarter))
        blk = jnp.concatenate([r, r, c, c], axis=2)
        parts, prev = [], 0
        for lo, hi in lane_layout:
            if lo > prev:
                parts.append(jnp.full((rows, GRID_W, lo - prev), fill, F32))
            parts.append(blk)
            prev = hi
        if prev < LANES:
            parts.append(jnp.full((rows, GRID_W, LANES - prev), fill, F32))
        lat = jnp.concatenate(parts, axis=2).reshape(seq, LANES)
        return jnp.concatenate([lat, jnp.full((t - seq, LANES), fill, F32)], axis=0)

    return table(jnp.cos, 1.0), table(jnp.sin, 0.0)


def _rope_partner_perm(rot_dim):
    q = rot_dim // 4
    src = np.concatenate([np.arange(q, 2 * q), np.arange(0, q), np.arange(3 * q, 4 * q), np.arange(2 * q, 3 * q)])
    sign = np.concatenate([-np.ones(q), np.ones(q), -np.ones(q), np.ones(q)]).astype(np.float32)
    return src, sign


def _swa_mask(ntl):
    out = []
    for i in (0, 1, ntl - 1):
        ts = min(max(i - 1, 0), ntl - 3)
        kpos = ts * TB + np.arange(3 * TB)[:, None]
        qpos = i * TB + np.arange(TB)[None, :]
        out.append(np.where(np.abs(kpos - qpos) <= SWA_WINDOW, 0.0, NEG).astype(np.float32))
    return jnp.asarray(np.stack(out)[:, None])


def _na_bias(rpb, rows, ntl):
    nh = rpb.shape[0]
    rpt = TB // GRID_W
    kr = min(NA_WIN_ROWS, rows)
    c = np.arange(GRID_W)
    cs = np.clip(c - NA_WIN_COLS // 2, 0, GRID_W - NA_WIN_COLS)
    okc = (c[:, None] >= cs[None, :]) & (c[:, None] < cs[None, :] + NA_WIN_COLS)
    dc = np.clip(c[:, None] - c[None, :] + NA_WIN_COLS - 1, 0, 2 * NA_WIN_COLS - 2)
    toep = jnp.where(okc[None, None], rpb[:, :, dc] * LOG2E, NEG)
    neg = jnp.full((nh, GRID_W, GRID_W), NEG, F32)
    out = []
    for i in (0, 1, ntl - 1):
        ts = min(max(i - 1, 0), ntl - 3)
        key_rows = []
        for krow in range(3 * rpt):
            r2 = ts * rpt + krow
            blocks = []
            for qrow in range(rpt):
                r = i * rpt + qrow
                rs = min(max(r - kr // 2, 0), rows - kr)
                blocks.append(toep[:, r2 - r + NA_WIN_ROWS - 1] if rs <= r2 < rs + kr else neg)
            key_rows.append(jnp.concatenate(blocks, axis=2))
        out.append(jnp.concatenate(key_rows, axis=1))
    return jnp.stack(out, axis=0).astype(F32)


def _mixer_na(xa, g, mods, w_qkv, w_o, rpb, ntl, seq):
    d = xa.shape[1]
    hd = d // NA_HEADS
    n = NA_HEADS * hd
    scale = hd ** -0.5 * LOG2E
    colscale = jnp.concatenate([jnp.full((n,), scale, F32), jnp.ones((2 * n,), F32)])
    qt, k, vt = _proj(xa, w_qkv.astype(BF16), ntl=ntl, mods=mods, norm_g=g, mod_idx=(0, 1),
                      colscale=colscale, splits=[(0, n), (n, 2 * n), (2 * n, 3 * n)],
                      out_t=[True, False, True], name="na_qkv")
    bias = _na_bias(rpb, seq // GRID_W, ntl)
    o = _attn("na", qt, k, vt, dk=hd, dv=hd, ntl=ntl, bias=bias)
    (xa,) = _proj(o, w_o.astype(BF16), ntl=ntl, mods=mods, resid=xa, gate_idx=2,
                  out_dtypes=[F32], name="na_out")
    return xa


def _mixer_swa(xa, g, mods, w_qkv, w_o, sink, ntl, seq):
    t, d = xa.shape
    hd = d // SWA_Q_HEADS
    nq, nkv = SWA_Q_HEADS * hd, SWA_KV_HEADS * hd
    scale = hd ** -0.5 * LOG2E
    nrope = nq + nkv
    src, sign = _rope_partner_perm(hd)
    nblk = nrope // hd
    src_full = (np.arange(nblk)[:, None] * hd + src[None, :]).reshape(-1)
    sign_full = np.tile(sign, nblk)
    w2 = w_qkv[:, src_full] * sign_full[None, :]
    cos, sin = _rope_tables(seq, t, hd, [(0, hd), (hd, 2 * hd)])
    colscale = jnp.concatenate([jnp.full((nq,), scale, F32), jnp.ones((2 * nkv,), F32)])
    qt, k, vt = _proj(xa, w_qkv.astype(BF16), ntl=ntl, mods=mods, norm_g=g, mod_idx=(0, 1),
                      w2=w2.astype(BF16), cos=cos, sin=sin, colscale=colscale,
                      splits=[(0, nq), (nq, nrope), (nrope, nrope + nkv)],
                      out_t=[True, False, True], name="swa_qkv")
    k = jnp.transpose(k.reshape(t, SWA_KV_HEADS, hd), (1, 0, 2))
    o = _attn("swa", qt, k, vt, dk=hd, dv=hd, ntl=ntl, bias=_swa_mask(ntl), sink=sink.astype(F32) * LOG2E)
    (xa,) = _proj(o, w_o.astype(BF16), ntl=ntl, mods=mods, resid=xa, gate_idx=2,
                  out_dtypes=[F32], name="swa_out")
    return xa


def _mixer_mla(xa, g, mods, w_dq, q_norm, w_uq, w_dkv, kv_norm, w_ukv, w_o, ntl, seq):
    t, d = xa.shape
    nh = MLA_HEADS
    qr = w_dq.shape[1]
    kvr = kv_norm.shape[0]
    nope, rope, dv = MLA_NOPE_DIM, MLA_ROPE_DIM, MLA_V_DIM
    qk = nope + rope
    pad = LANES - qk
    scale = qk ** -0.5 * LOG2E
    kin = kvr + LANES
    w_dn = jnp.concatenate([w_dq, w_dkv, jnp.zeros((d, LANES - rope), F32)], axis=1)
    g_dn = jnp.concatenate([q_norm, kv_norm, jnp.ones((LANES,), F32)])
    cqn, kvin = _proj(xa, w_dn.astype(BF16), ntl=ntl, mods=mods, norm_g=g, mod_idx=(0, 1),
                      rms_segs=[(0, qr), (qr, qr + kvr)], rms_g=g_dn,
                      splits=[(0, qr), (qr, qr + kin)], name="mla_down")
    src, sign = _rope_partner_perm(rope)
    src_head = np.concatenate([np.arange(nope), nope + src, np.arange(qk, LANES)])
    sign_head = np.concatenate([np.zeros(nope, np.float32), sign, np.zeros(pad, np.float32)])
    src_full = (np.arange(nh)[:, None] * LANES + src_head[None, :]).reshape(-1)
    sign_full = np.tile(sign_head, nh)
    cos, sin = _rope_tables(seq, t, rope, [(nope, qk)])
    wq = jnp.pad(w_uq.reshape(qr, nh, qk), ((0, 0), (0, 0), (0, pad))).reshape(qr, nh * LANES)
    wq2 = wq[:, src_full] * sign_full[None, :]
    (qt,) = _proj(cqn, wq.astype(BF16), ntl=ntl, w2=wq2.astype(BF16), cos=cos, sin=sin,
                  colscale=jnp.full((nh * LANES,), scale, F32), out_t=[True], name="mla_q")
    w_ukv3 = w_ukv.reshape(kvr, nh, nope + dv)
    wk_top = jnp.pad(w_ukv3[:, :, :nope], ((0, 0), (0, 0), (0, LANES - nope)))
    eye = jnp.eye(rope, dtype=F32)
    wk_rope = jnp.pad(eye, ((0, LANES - rope), (nope, pad)))
    wk_bot = jnp.broadcast_to(wk_rope[:, None, :], (LANES, nh, LANES))
    wk = jnp.concatenate([wk_top, wk_bot], axis=0).reshape(kin, nh * LANES)
    wk2 = wk[:, src_full] * sign_full[None, :]
    wv = jnp.concatenate([w_ukv3[:, :, nope:].reshape(kvr, nh * dv), jnp.zeros((LANES, nh * dv), F32)], axis=0)
    (k,) = _proj(kvin, wk.astype(BF16), ntl=ntl, w2=wk2.astype(BF16), cos=cos, sin=sin, name="mla_k")
    (vt,) = _proj(kvin, wv.astype(BF16), ntl=ntl, out_t=[True], name="mla_v")
    o = _attn("mla", qt, k, vt, dk=LANES, dv=dv, ntl=ntl)
    (xa,) = _proj(o, w_o.astype(BF16), ntl=ntl, mods=mods, resid=xa, gate_idx=2,
                  out_dtypes=[F32], name="mla_out")
    return xa


def _moe(xa, g, mods, router, w_gate, w_up, w_down, layer, ntl, seq, with_ctx):
    t, d = xa.shape
    ne = router.shape[1]
    nctx = t - seq
    h, aff = _router(xa, g, mods, router.T, ntl)
    cap_l = EC_CAPACITY_FACTOR * seq // ne
    pos_l, gate_l, off_l = _topk(aff[:, :seq].reshape(ne, seq // CHUNK, CHUNK), cap_l, 0)
    per_tile = TB // CHUNK
    pos = [pos_l.reshape(ne, seq)]
    gate = [gate_l.reshape(ne, seq)]
    offs = [off_l[:, ::per_tile, 0]]
    if with_ctx:
        cap_c = EC_CAPACITY_FACTOR * nctx // ne
        cpad = 8 * CHUNK
        aff_c = jnp.concatenate([aff[:, seq:], jnp.full((ne, cpad - nctx), -1.0, F32)], axis=1)
        pos_c, gate_c, off_c = _topk(aff_c.reshape(ne, 8, CHUNK), cap_c, cap_l)
        pos.append(pos_c.reshape(ne, cpad)[:, :nctx])
        gate.append(gate_c.reshape(ne, cpad)[:, :nctx])
        offs.append(off_c[:, 0:nctx // CHUNK:per_tile, 0])
        nslots = cap_l + cap_c
    else:
        pos.append(jnp.full((ne, nctx), -1.0, F32))
        gate.append(jnp.zeros((ne, nctx), F32))
        offs.append(jnp.full((ne, nctx // TB), cap_l, jnp.int32))
        nslots = cap_l
    pos = jnp.concatenate(pos, axis=1)
    gate = jnp.concatenate(gate, axis=1)
    offs = jnp.concatenate(offs + [jnp.full((ne, 1), nslots, jnp.int32)], axis=1)
    cnts = offs[:, 1:] - offs[:, :-1]
    ys = _ffn(offs, pos, gate, h, w_gate, w_up, w_down, layer=layer, nslots=nslots)
    return _combine(offs, cnts, pos.T, xa, mods, ys, ntl)


def kernel(x, c, ctx, c_ctx, ada_w, ada_b, norm_mix, norm_ffn, na_w_qkv, na_w_o, na_rpb, swa_w_qkv, swa_w_o, swa_sink, mla_w_dq, mla_q_norm, mla_w_uq, mla_w_dkv, mla_kv_norm, mla_w_ukv, mla_w_o, moe_router, moe_w_gate, moe_w_up, moe_w_down, final_norm):
    assert x.shape[0] == 1 and c.shape[0] == 1 and ctx.shape[0] == 1
    seq, d = x.shape[1], x.shape[2]
    nctx = ctx.shape[1]
    assert seq % TB == 0 and nctx == TB and seq // TB >= 4
    depth = ada_w.shape[0]
    ntl = seq // TB
    xa = jnp.concatenate([x[0], ctx[0]], axis=0)
    cs = jnp.concatenate([c, c_ctx[None, :], jnp.zeros((6, d), F32)], axis=0)
    ada = _ada(cs, ada_w, ada_b)
    mods_all = jnp.transpose(ada[:, :2].reshape(depth, 2, 6, d), (0, 2, 1, 3))[:, :, :, None, :]
    for i in range(depth):
        mods = mods_all[i]
        kind, slot = i % N_MIXERS, i // N_MIXERS
        if kind == 0:
            xa = _mixer_na(xa, norm_mix[i], mods, na_w_qkv[slot], na_w_o[slot], na_rpb[slot], ntl, seq)
        elif kind == 1:
            xa = _mixer_swa(xa, norm_mix[i], mods, swa_w_qkv[slot], swa_w_o[slot], swa_sink[slot], ntl, seq)
        else:
            xa = _mixer_mla(xa, norm_mix[i], mods, mla_w_dq[slot], mla_q_norm[slot], mla_w_uq[slot],
                            mla_w_dkv[slot], mla_kv_norm[slot], mla_w_ukv[slot], mla_w_o[slot], ntl, seq)
        xa = _moe(xa, norm_ffn[i], mods, moe_router[i], moe_w_gate, moe_w_up, moe_w_down, i,
                  ntl, seq, with_ctx=i < depth - 1)
    return _final_norm(xa, final_norm, seq)[None]
```

```python
import functools
import math

import jax
import jax.numpy as jnp
import numpy as np
from jax import lax
from jax.experimental import pallas as pl
from jax.experimental.pallas import tpu as pltpu

F32 = jnp.float32
BF16 = jnp.bfloat16
HIGHEST = lax.Precision.HIGHEST

GRID_W = 64
N_MIXERS = 3
RMS_EPS = 1e-6
ROPE_THETA = 10000.0
NA_HEADS = 16
NA_WIN_ROWS = 8
NA_WIN_COLS = 16
SWA_Q_HEADS = 16
SWA_KV_HEADS = 4
SWA_WINDOW = 128
MLA_HEADS = 16
MLA_NOPE_DIM = 64
MLA_ROPE_DIM = 32
MLA_V_DIM = 64
N_EXPERTS = 16
EC_CAPACITY_FACTOR = 2

TB = 256
CHUNK = 128
LANES = 128
NEG = -1e30
LOG2E = math.log2(math.e)
FFN_ROWS = 416
FFN_FCHUNK = 512
GATHER_WIN = TB + 16
COMBINE_WIN = 64
VMEM_LIMIT = 56 * 1024 * 1024


def _cparams(*sem):
    return pltpu.CompilerParams(dimension_semantics=sem, vmem_limit_bytes=VMEM_LIMIT)


def _ada_kernel(cs_ref, w_ref, b_ref, o_ref):
    x = cs_ref[...]
    x = x * jax.nn.sigmoid(x)
    y = jnp.dot(x, w_ref[0], precision=HIGHEST, preferred_element_type=F32)
    o_ref[0] = y + b_ref[0]


def _ada(cs, ada_w, ada_b):
    depth, d, n = ada_w.shape
    nb = n // d
    return pl.pallas_call(
        _ada_kernel,
        grid=(depth, nb),
        in_specs=[
            pl.BlockSpec((8, d), lambda i, k: (0, 0)),
            pl.BlockSpec((1, d, d), lambda i, k: (i, 0, k)),
            pl.BlockSpec((1, 1, d), lambda i, k: (i, 0, k)),
        ],
        out_specs=pl.BlockSpec((1, 8, d), lambda i, k: (i, 0, k)),
        out_shape=jax.ShapeDtypeStruct((depth, 8, n), F32),
        compiler_params=_cparams("arbitrary", "arbitrary"),
        name="ada",
    )(cs, ada_w, ada_b.reshape(depth, 1, n))


def _normmod(x, g, shift, scale):
    y = x * lax.rsqrt(jnp.mean(x * x, axis=-1, keepdims=True) + RMS_EPS)
    return (y * g) * (1.0 + scale) + shift


def _proj_kernel(*refs, normmod, rms_segs, rope_cols, colscale, resid, splits, out_t):
    it = iter(refs)
    x_ref = next(it)
    if normmod:
        g_ref, sh_ref, sc_ref = next(it), next(it), next(it)
    w_ref = next(it)
    if rms_segs:
        g2_ref = next(it)
    if rope_cols:
        w2_ref, cos_ref, sin_ref = next(it), next(it), next(it)
    if colscale:
        cs_ref = next(it)
    if resid:
        res_ref, gate_ref = next(it), next(it)
    out_refs = list(it)

    x = x_ref[...]
    if normmod:
        x = _normmod(x.astype(F32), g_ref[...], sh_ref[0, 0], sc_ref[0, 0])
    xb = x.astype(BF16)
    y = jnp.dot(xb, w_ref[...], preferred_element_type=F32)
    if rms_segs:
        parts = []
        prev = 0
        for (a, b) in rms_segs:
            if a > prev:
                parts.append(y[:, prev:a])
            seg = y[:, a:b]
            seg = seg * lax.rsqrt(jnp.mean(seg * seg, axis=-1, keepdims=True) + RMS_EPS)
            parts.append(seg * g2_ref[:, a:b])
            prev = b
        if prev < y.shape[1]:
            parts.append(y[:, prev:])
        y = jnp.concatenate(parts, axis=1)
    if rope_cols:
        y2 = jnp.dot(xb, w2_ref[...], preferred_element_type=F32)
        reps = rope_cols // LANES
        cos = jnp.tile(cos_ref[...], (1, reps))
        sin = jnp.tile(sin_ref[...], (1, reps))
        yr = y[:, :rope_cols] * cos + y2 * sin
        y = yr if rope_cols == y.shape[1] else jnp.concatenate([yr, y[:, rope_cols:]], axis=1)
    if colscale:
        y = y * cs_ref[...]
    if resid:
        y = res_ref[...] + gate_ref[0, 0] * y
    for o_ref, (a, b), tr in zip(out_refs, splits, out_t):
        if tr:
            o_ref[0] = y[:, a:b].T.astype(o_ref.dtype)
        else:
            o_ref[...] = y[:, a:b].astype(o_ref.dtype)


def _proj(x, w, *, ntl, mods=None, norm_g=None, mod_idx=None, rms_segs=None, rms_g=None,
          w2=None, cos=None, sin=None, colscale=None, resid=None, gate_idx=None,
          splits=None, out_dtypes=None, out_t=None, name="proj"):
    t, kdim = x.shape
    n = w.shape[1]
    nt = t // TB
    splits = splits or [(0, n)]
    out_dtypes = out_dtypes or [BF16] * len(splits)

    def modspec(k):
        dm = mods.shape[-1]
        return pl.BlockSpec((1, 1, 1, dm), lambda i: (k, i // ntl, 0, 0))

    args, specs = [x], [pl.BlockSpec((TB, kdim), lambda i: (i, 0))]
    if norm_g is not None:
        args += [norm_g.reshape(1, kdim), mods, mods]
        specs += [pl.BlockSpec((1, kdim), lambda i: (0, 0)), modspec(mod_idx[0]), modspec(mod_idx[1])]
    args.append(w)
    specs.append(pl.BlockSpec((kdim, n), lambda i: (0, 0)))
    if rms_segs:
        args.append(rms_g.reshape(1, n))
        specs.append(pl.BlockSpec((1, n), lambda i: (0, 0)))
    rope_cols = 0
    if w2 is not None:
        rope_cols = w2.shape[1]
        args += [w2, cos, sin]
        specs += [pl.BlockSpec((kdim, rope_cols), lambda i: (0, 0)),
                  pl.BlockSpec((TB, LANES), lambda i: (i, 0)),
                  pl.BlockSpec((TB, LANES), lambda i: (i, 0))]
    if colscale is not None:
        args.append(colscale.reshape(1, n))
        specs.append(pl.BlockSpec((1, n), lambda i: (0, 0)))
    if resid is not None:
        args += [resid, mods]
        specs += [pl.BlockSpec((TB, n), lambda i: (i, 0)), modspec(gate_idx)]
    out_t = out_t or [False] * len(splits)
    out_shape = [jax.ShapeDtypeStruct((nt, b - a, TB) if tr else (t, b - a), dt)
                 for (a, b), dt, tr in zip(splits, out_dtypes, out_t)]
    out_specs = [pl.BlockSpec((1, b - a, TB), lambda i: (i, 0, 0)) if tr
                 else pl.BlockSpec((TB, b - a), lambda i: (i, 0)) for (a, b), tr in zip(splits, out_t)]
    kern = functools.partial(
        _proj_kernel, normmod=norm_g is not None, rms_segs=rms_segs, rope_cols=rope_cols,
        colscale=colscale is not None, resid=resid is not None, splits=splits, out_t=out_t)
    outs = pl.pallas_call(
        kern, grid=(nt,), in_specs=specs, out_specs=out_specs, out_shape=out_shape,
        compiler_params=_cparams("arbitrary"), name=name,
    )(*args)
    return outs


ONES_ROWS = 16


def _softmax_part(st, m):
    m_new = jnp.maximum(m, jnp.max(st, axis=0, keepdims=True))
    return m_new, jnp.exp2(m - m_new), jnp.exp2(st - m_new).astype(BF16)


def _pv_part(vt, p, alpha, acc):
    return alpha * acc + jnp.dot(vt, p, preferred_element_type=F32)


def _softmax_step(st, vt, carry):
    m, acc = carry
    m, alpha, p = _softmax_part(st, m)
    return m, _pv_part(vt, p, alpha, acc)


def _softmax_init(dv):
    return (jnp.full((1, TB), NEG, F32), jnp.zeros((dv + ONES_ROWS, TB), F32))


def _softmax_finish(carry, dv, sink=None):
    m, acc = carry
    num, l = acc[:dv], acc[dv:dv + 1]
    if sink is not None:
        m_f = jnp.maximum(m, sink)
        a = jnp.exp2(m - m_f)
        l = l * a + jnp.exp2(sink - m_f)
        num = num * a
    return (num / l).T


def _scores_t(k, qt):
    return jnp.dot(k, qt, preferred_element_type=F32)


def _attn_kernel(*refs, kind, dk, dv, ntl, hps):
    heads = range(hps)
    if kind == "na":
        q_ref, k_ref, vt_ref, bias_ref, o_ref, *scratch = refs
    elif kind == "swa":
        sink_ref, q_ref, k_ref, vt_ref, bias_ref, o_ref, *scratch = refs
    else:
        q_ref, k_ref, vt_ref, o_ref, *scratch = refs
    hp = pl.program_id(0)
    i = pl.program_id(1)
    shared_kv = kind == "swa"
    qs = [q_ref[0, hh * dk:(hh + 1) * dk, :] for hh in heads]
    ones = jnp.ones((ONES_ROWS, TB), BF16)

    def scores(hh, kt):
        start = pl.multiple_of(kt * TB, TB)
        if shared_kv:
            return _scores_t(k_ref[0, pl.ds(start, TB), :], qs[hh])
        return _scores_t(k_ref[pl.ds(start, TB), hh * dk:(hh + 1) * dk], qs[hh])

    def vt_tile(hh, kt):
        lo = 0 if shared_kv else hh * dv
        return jnp.concatenate([vt_ref[kt, lo:lo + dv, :], ones], axis=0)

    sa_ref, sb_ref, pa_ref, pb_ref = scratch
    states = tuple((m, jnp.ones((1, TB), F32), acc) for m, acc in (_softmax_init(dv) for _ in heads))

    def stage(score_next, kt_prev, s_cur, s_nxt, p_cur, p_prev, states):
        if score_next is not None:
            for hh in heads:
                s_nxt[hh] = score_next(hh)
        new = []
        for hh in heads:
            m, alpha, acc = states[hh]
            if p_prev is not None:
                acc = _pv_part(vt_tile(hh, kt_prev), p_prev[hh], alpha, acc)
            m, alpha, p = _softmax_part(s_cur[hh], m)
            p_cur[hh] = p
            new.append((m, alpha, acc))
        return tuple(new)

    if kind == "mla":
        def latent(kt):
            return lambda hh: scores(hh, jnp.minimum(kt, ntl))

        def body(j, states):
            states = stage(latent(2 * j + 2), 2 * j, sb_ref, sa_ref, pb_ref, pa_ref, states)
            return stage(latent(2 * j + 3), 2 * j + 1, sa_ref, sb_ref, pa_ref, pb_ref, states)

        for hh in heads:
            sa_ref[hh] = scores(hh, 0)
        states = stage(latent(1), None, sa_ref, sb_ref, pa_ref, None, states)
        states = lax.fori_loop(0, ntl // 2, body, states, unroll=4 if ntl % 8 == 0 else 1)
        last_tile = ntl
    else:
        ts = jnp.clip(i - 1, 0, ntl - 3)
        tiles = [ts, ts + 1, ts + 2, ntl]

        def masked(s):
            def fn(hh):
                st = scores(hh, tiles[s])
                if s == 3:
                    return st
                return st + bias_ref[0, hh if kind == "na" else 0, s * TB:(s + 1) * TB, :]
            return fn

        for hh in heads:
            sa_ref[hh] = masked(0)(hh)
        states = stage(masked(1), None, sa_ref, sb_ref, pa_ref, None, states)
        states = stage(masked(2), tiles[0], sb_ref, sa_ref, pb_ref, pa_ref, states)
        states = stage(masked(3), tiles[1], sa_ref, sb_ref, pa_ref, pb_ref, states)
        states = stage(None, tiles[2], sb_ref, sa_ref, pb_ref, pa_ref, states)
        last_tile = tiles[3]
    last_p = pa_ref if kind == "mla" else pb_ref
    carry = [(m, _pv_part(vt_tile(hh, last_tile), last_p[hh], alpha, acc))
             for hh, (m, alpha, acc) in zip(heads, states)]
    outs = []
    for hh in heads:
        sink = sink_ref[hps * hp + hh] if kind == "swa" else None
        outs.append(_softmax_finish(carry[hh], dv, sink))
    o_ref[...] = jnp.concatenate(outs, axis=1).astype(o_ref.dtype)


def _attn_ctx_kernel(*refs, dk, dv, hps, has_sink, shared_kv):
    if has_sink:
        sink_ref, q_ref, k_ref, vt_ref, _, o_ref = refs
    else:
        q_ref, k_ref, vt_ref, _, o_ref = refs
    hp = pl.program_id(0)
    ones = jnp.ones((ONES_ROWS, TB), BF16)
    outs = []
    for hh in range(hps):
        k = k_ref[0] if shared_kv else k_ref[:, hh * dk:(hh + 1) * dk]
        lo = 0 if shared_kv else hh * dv
        st = _scores_t(k, q_ref[0, hh * dk:(hh + 1) * dk, :])
        vt = jnp.concatenate([vt_ref[0, lo:lo + dv, :], ones], axis=0)
        carry = _softmax_step(st, vt, _softmax_init(dv))
        outs.append(_softmax_finish(carry, dv, sink_ref[hps * hp + hh] if has_sink else None))
    o_ref[...] = jnp.concatenate(outs, axis=1).astype(o_ref.dtype)


def _attn(kind, qt, k, vt, *, dk, dv, ntl, bias=None, sink=None):
    nt, _, _ = qt.shape
    t = nt * TB
    nh = qt.shape[1] // dk
    shared_kv = kind == "swa"
    hps = 2 if kind == "mla" else 4
    kvs = 1 if shared_kv else hps
    assert not shared_kv or nh // k.shape[0] == hps
    kern = functools.partial(_attn_kernel, kind=kind, dk=dk, dv=dv, ntl=ntl, hps=hps)
    in_specs = [
        pl.BlockSpec((1, hps * dk, TB), lambda hp, i, *_: (i, hp, 0)),
        pl.BlockSpec((1, t, dk), lambda hp, i, *_: (hp, 0, 0)) if shared_kv
        else pl.BlockSpec((t, hps * dk), lambda hp, i, *_: (0, hp)),
        pl.BlockSpec((nt, kvs * dv, TB), lambda hp, i, *_: (0, hp, 0)),
    ]
    assert ntl % 2 == 0
    args = [qt, k, vt]
    nsp = 0
    if kind in ("na", "swa"):
        per_head = kind == "na"

        def bias_map(hp, i, *_):
            pat = jnp.where(i == 0, 0, jnp.where(i >= ntl - 1, 2, 1))
            return (pat, hp if per_head else 0, 0, 0)
        in_specs.append(pl.BlockSpec((1, hps if per_head else 1, 3 * TB, TB), bias_map))
        args.append(bias)
    if kind == "swa":
        nsp = 1
        args = [sink] + args
    o = pl.pallas_call(
        kern,
        grid_spec=pltpu.PrefetchScalarGridSpec(
            num_scalar_prefetch=nsp, grid=(nh // hps, ntl), in_specs=in_specs,
            out_specs=pl.BlockSpec((TB, hps * dv), lambda hp, i, *_: (i, hp)),
            scratch_shapes=[pltpu.VMEM((hps, TB, TB), F32)] * 2 + [pltpu.VMEM((hps, TB, TB), BF16)] * 2),
        out_shape=jax.ShapeDtypeStruct((t, nh * dv), BF16),
        compiler_params=_cparams("arbitrary", "arbitrary"), name="attn_" + kind,
    )(*args)
    has_sink = kind == "swa"
    ckern = functools.partial(_attn_ctx_kernel, dk=dk, dv=dv, hps=hps, has_sink=has_sink,
                              shared_kv=shared_kv)
    cargs = ([sink] if has_sink else []) + [qt, k, vt, o]
    return pl.pallas_call(
        ckern,
        grid_spec=pltpu.PrefetchScalarGridSpec(
            num_scalar_prefetch=nsp, grid=(nh // hps,),
            in_specs=[pl.BlockSpec((1, hps * dk, TB), lambda hp, *_: (ntl, hp, 0)),
                      pl.BlockSpec((1, TB, dk), lambda hp, *_: (hp, ntl, 0)) if shared_kv
                      else pl.BlockSpec((TB, hps * dk), lambda hp, *_: (ntl, hp)),
                      pl.BlockSpec((1, kvs * dv, TB), lambda hp, *_: (ntl, hp, 0)),
                      pl.BlockSpec(memory_space=pl.ANY)],
            out_specs=pl.BlockSpec((TB, hps * dv), lambda hp, *_: (ntl, hp))),
        out_shape=jax.ShapeDtypeStruct((t, nh * dv), BF16),
        input_output_aliases={len(cargs) - 1: 0},
        compiler_params=_cparams("arbitrary"), name="attn_ctx_" + kind,
    )(*cargs)


def _router_kernel(x_ref, g_ref, sh_ref, sc_ref, rt_ref, h_ref, aff_ref):
    h = _normmod(x_ref[...], g_ref[...], sh_ref[0, 0], sc_ref[0, 0])
    h_ref[...] = h.astype(BF16)
    lg = lax.dot_general(rt_ref[...], h, (((1,), (1,)), ((), ())),
                         precision=HIGHEST, preferred_element_type=F32)
    lg = lg - jnp.max(lg, axis=0, keepdims=True)
    e = jnp.exp(lg)
    aff_ref[...] = e / jnp.sum(e, axis=0, keepdims=True)


def _router(xa, g, mods, router_t, ntl):
    t, d = xa.shape
    ne = router_t.shape[0]

    def modspec(k):
        return pl.BlockSpec((1, 1, 1, d), lambda i: (k, i // ntl, 0, 0))

    return pl.pallas_call(
        _router_kernel, grid=(t // TB,),
        in_specs=[pl.BlockSpec((TB, d), lambda i: (i, 0)),
                  pl.BlockSpec((1, d), lambda i: (0, 0)),
                  modspec(3), modspec(4),
                  pl.BlockSpec((ne, d), lambda i: (0, 0))],
        out_specs=[pl.BlockSpec((TB, d), lambda i: (i, 0)),
                   pl.BlockSpec((ne, TB), lambda i: (0, i))],
        out_shape=[jax.ShapeDtypeStruct((t, d), BF16), jax.ShapeDtypeStruct((ne, t), F32)],
        compiler_params=_cparams("arbitrary"), name="router",
    )(xa, g.reshape(1, d), mods, mods, router_t)


def _topk_kernel(aff_ref, pos_ref, gate_ref, off_ref, *, cap, base):
    x = aff_ref[...]
    ne, nc, _ = x.shape
    bits = lax.bitcast_convert_type(x, jnp.int32)

    def count(mask):
        c = jnp.sum(jnp.where(mask, 1.0, 0.0), axis=1, keepdims=True)
        return jnp.sum(c, axis=2, keepdims=True)

    thr = jnp.zeros((ne, 1, 1), jnp.int32)
    for b in range(30, -1, -1):
        cand = thr | (1 << b)
        thr = jnp.where(count(bits >= cand) >= cap, cand, thr)

    ia = lax.broadcasted_iota(jnp.int32, (LANES, LANES), 0)
    ib = lax.broadcasted_iota(jnp.int32, (LANES, LANES), 1)
    upper = jnp.where(ia <= ib, 1.0, 0.0)
    ones = jnp.ones((LANES, LANES), F32)
    ca = lax.broadcasted_iota(jnp.int32, (nc, nc), 0)
    cb = lax.broadcasted_iota(jnp.int32, (nc, nc), 1)
    lower = jnp.where(cb < ca, 1.0, 0.0)

    def prefix(mask):
        m2 = jnp.where(mask, 1.0, 0.0).reshape(ne * nc, LANES)
        within = jnp.dot(m2, upper, precision=HIGHEST, preferred_element_type=F32)
        tot = jnp.dot(m2, ones, precision=HIGHEST, preferred_element_type=F32).reshape(ne, nc, LANES)
        offs = [jnp.dot(lower, tot[e], precision=HIGHEST, preferred_element_type=F32)[None]
                for e in range(ne)]
        off = jnp.concatenate(offs, axis=0)
        return within.reshape(ne, nc, LANES) + off, off

    gt = bits > thr
    eq = bits == thr
    need = cap - count(gt)
    eq_rank, _ = prefix(eq)
    sel = gt | (eq & (eq_rank <= need))
    sel_rank, off = prefix(sel)
    pos_ref[...] = jnp.where(sel, sel_rank - 1.0 + base, -1.0)
    gate_ref[...] = jnp.where(sel, x, 0.0)
    off_ref[...] = off.astype(jnp.int32) + base


def _topk(aff3, cap, base):
    ne, nc, _ = aff3.shape
    kern = functools.partial(_topk_kernel, cap=cap, base=base)
    spec = pl.BlockSpec((ne, nc, LANES), lambda i: (0, 0, 0))
    return pl.pallas_call(
        kern, grid=(1,), in_specs=[spec], out_specs=[spec, spec, spec],
        out_shape=[jax.ShapeDtypeStruct(aff3.shape, F32), jax.ShapeDtypeStruct(aff3.shape, F32),
                   jax.ShapeDtypeStruct(aff3.shape, jnp.int32)],
        compiler_params=_cparams("arbitrary"), name="topk",
    )(aff3)


def _ffn_kernel(offs_ref, pos_ref, gate_ref, h_ref, wg_ref, wu_ref, wd_ref, y_ref,
                xs_ref, xb_ref, gs_ref, *, ng, gt, nf, nrows):
    e = pl.program_id(0)
    j = pl.program_id(1)

    @pl.when(j == 0)
    def _():
        xs_ref[...] = jnp.zeros_like(xs_ref)
        gs_ref[...] = jnp.zeros_like(gs_ref)

    @pl.when(j < ng)
    def _():
        for s in range(gt):
            off = offs_ref[e, j * gt + s]
            w0 = pl.multiple_of((off // 8) * 8, 8)
            prow = pos_ref[0, :, s * TB:(s + 1) * TB]
            grow = gate_ref[0, :, s * TB:(s + 1) * TB]
            ids = (w0 + lax.broadcasted_iota(jnp.int32, (GATHER_WIN, 1), 0)).astype(F32)
            hit = prow == ids
            onehot = jnp.where(hit, 1.0, 0.0).astype(BF16)
            xs_ref[pl.ds(w0, GATHER_WIN), :] += jnp.dot(
                onehot, h_ref[s * TB:(s + 1) * TB, :], preferred_element_type=F32)
            gsel = jnp.sum(jnp.where(hit, grow, 0.0), axis=1, keepdims=True)
            gs_ref[pl.ds(w0, GATHER_WIN), :] += jnp.broadcast_to(gsel, (GATHER_WIN, LANES))

    @pl.when(j == ng)
    def _():
        xb_ref[...] = xs_ref[0:nrows, :].astype(BF16)
        xs_ref[...] = jnp.zeros_like(xs_ref)

    @pl.when(j >= ng)
    def _():
        wg = wg_ref[0, 0].astype(BF16)
        wu = wu_ref[0, 0].astype(BF16)
        wd = wd_ref[0, 0].astype(BF16)
        for b in range(nrows // FFN_ROWS):
            rows = slice(b * FFN_ROWS, (b + 1) * FFN_ROWS)
            x = xb_ref[rows, :]
            a = jnp.dot(x, wg, preferred_element_type=F32)
            u = jnp.dot(x, wu, preferred_element_type=F32)
            hmid = (a * jax.nn.sigmoid(a) * u).astype(BF16)
            xs_ref[rows, :] += jnp.dot(hmid, wd, preferred_element_type=F32)

    @pl.when(j == ng + nf - 1)
    def _():
        y_ref[0, 0:nrows, :] = (xs_ref[0:nrows, :] * gs_ref[0:nrows, 0:1]).astype(y_ref.dtype)
        if y_ref.shape[1] > nrows:
            y_ref[0, nrows:, :] = jnp.zeros((y_ref.shape[1] - nrows, y_ref.shape[2]), y_ref.dtype)


def _ffn(offs, pos, gate, h, wg, wu, wd, *, layer, nslots):
    t, d = h.shape
    nt = t // TB
    _, ne, _, f = wg.shape
    gt = max(g for g in range(1, 9) if nt % g == 0)
    ng = nt // gt
    nf = f // FFN_FCHUNK
    nrows = -(-nslots // FFN_ROWS) * FFN_ROWS
    srows = -(-(nslots + TB) // 16) * 16
    srows = max(srows, nrows)
    xs_rows = max(nrows, nslots + GATHER_WIN)
    kern = functools.partial(_ffn_kernel, ng=ng, gt=gt, nf=nf, nrows=nrows)

    def fchunk(j):
        return jnp.clip(j - ng, 0, nf - 1)

    grid_spec = pltpu.PrefetchScalarGridSpec(
        num_scalar_prefetch=1, grid=(ne, ng + nf),
        in_specs=[
            pl.BlockSpec((1, 1, gt * TB), lambda e, j, o: (e, 0, jnp.minimum(j, ng - 1))),
            pl.BlockSpec((1, 1, gt * TB), lambda e, j, o: (e, 0, jnp.minimum(j, ng - 1))),
            pl.BlockSpec((gt * TB, d), lambda e, j, o: (jnp.minimum(j, ng - 1), 0)),
            pl.BlockSpec((1, 1, d, FFN_FCHUNK), lambda e, j, o: (layer, e, 0, fchunk(j))),
            pl.BlockSpec((1, 1, d, FFN_FCHUNK), lambda e, j, o: (layer, e, 0, fchunk(j))),
            pl.BlockSpec((1, 1, FFN_FCHUNK, d), lambda e, j, o: (layer, e, fchunk(j), 0)),
        ],
        out_specs=pl.BlockSpec((1, srows, d), lambda e, j, o: (e, 0, 0)),
        scratch_shapes=[pltpu.VMEM((xs_rows, d), F32), pltpu.VMEM((nrows, d), BF16),
                        pltpu.VMEM((xs_rows, LANES), F32)])
    return pl.pallas_call(
        kern, grid_spec=grid_spec,
        out_shape=jax.ShapeDtypeStruct((ne, srows, d), BF16),
        compiler_params=_cparams("arbitrary", "arbitrary"), name="ffn",
    )(offs, pos.reshape(ne, 1, t), gate.reshape(ne, 1, t), h, wg, wu, wd)


def _combine_kernel(*refs, ne, srows):
    offs_ref, cnt_ref = refs[0], refs[1]
    posc_ref, xa_ref, gate_ref = refs[2], refs[3], refs[4]
    win_refs = refs[5:5 + ne]
    win2_refs = refs[5 + ne:5 + 2 * ne]
    o_ref = refs[5 + 2 * ne]
    j = pl.program_id(0)
    lane = lax.broadcasted_iota(jnp.int32, (1, LANES), 1)
    first = lane < COMBINE_WIN
    blocks = []
    for e in range(0, ne, 2):
        w0a = (offs_ref[e, j] // 16) * 16
        w0b = (offs_ref[e + 1, j] // 16) * 16
        ids = jnp.where(first, w0a + lane, w0b + lane - COMBINE_WIN).astype(F32)
        pcol = jnp.where(first, posc_ref[:, e:e + 1], posc_ref[:, e + 1:e + 2])
        blocks.append(jnp.where(pcol == ids, 1.0, 0.0).astype(BF16))
    onehot = jnp.concatenate(blocks, axis=1)
    ycat = jnp.concatenate([w[...] for w in win_refs], axis=0)
    acc = jnp.dot(onehot, ycat, preferred_element_type=F32)
    o_ref[...] = xa_ref[...] + gate_ref[0, 0] * acc

    wide = lax.broadcasted_iota(jnp.int32, (1, TB), 1)
    for e in range(ne):
        off = offs_ref[e, j]
        w0 = (off // 16) * 16

        @pl.when(off - w0 + cnt_ref[e, j] > COMBINE_WIN)
        def _(e=e, w0=w0):
            w1 = jnp.minimum(w0 + COMBINE_WIN, srows - TB)
            ids = w1 + wide
            hit = (posc_ref[:, e:e + 1] == ids.astype(F32)) & (ids >= w0 + COMBINE_WIN)
            oh = jnp.where(hit, 1.0, 0.0).astype(BF16)
            o_ref[...] += gate_ref[0, 0] * jnp.dot(oh, win2_refs[e][...], preferred_element_type=F32)


def _combine(offs, cnts, pos_cols, xa, mods, ys, ntl):
    t, d = xa.shape
    nt = t // TB
    ne, srows, _ = ys.shape
    assert srows % 16 == 0
    ys2 = ys.reshape(ne * srows, d)

    assert ne % 2 == 0 and 2 * COMBINE_WIN == LANES

    def win_spec(e):
        def imap(j, offs, cnts):
            return ((e * srows // 16 + offs[e, j] // 16) * 16, 0)
        return pl.BlockSpec((pl.Element(COMBINE_WIN), pl.Element(d)), imap)

    def win2_spec(e):
        def imap(j, offs, cnts):
            off = offs[e, j]
            w0 = (off // 16) * 16
            need = off - w0 + cnts[e, j] > COMBINE_WIN
            w1 = jnp.where(need, jnp.minimum(w0 + COMBINE_WIN, srows - TB), 0)
            return ((e * srows // 16 + w1 // 16) * 16, 0)
        return pl.BlockSpec((pl.Element(TB), pl.Element(d)), imap)

    in_specs = [
        pl.BlockSpec((TB, ne), lambda j, o, c: (j, 0)),
        pl.BlockSpec((TB, d), lambda j, o, c: (j, 0)),
        pl.BlockSpec((1, 1, 1, d), lambda j, o, c: (5, j // ntl, 0, 0)),
    ] + [win_spec(e) for e in range(ne)] + [win2_spec(e) for e in range(ne)]
    grid_spec = pltpu.PrefetchScalarGridSpec(
        num_scalar_prefetch=2, grid=(nt,), in_specs=in_specs,
        out_specs=pl.BlockSpec((TB, d), lambda j, o, c: (j, 0)))
    kern = functools.partial(_combine_kernel, ne=ne, srows=srows)
    return pl.pallas_call(
        kern, grid_spec=grid_spec, out_shape=jax.ShapeDtypeStruct((t, d), F32),
        compiler_params=_cparams("arbitrary"), name="combine",
    )(offs, cnts, pos_cols, xa, mods, *([ys2] * (2 * ne)))


def _final_kernel(x_ref, g_ref, o_ref):
    x = x_ref[...]
    o_ref[...] = x * lax.rsqrt(jnp.mean(x * x, axis=-1, keepdims=True) + RMS_EPS) * g_ref[...]


def _final_norm(xa, g, seq):
    d = xa.shape[1]
    return pl.pallas_call(
        _final_kernel, grid=(seq // TB,),
        in_specs=[pl.BlockSpec((TB, d), lambda i: (i, 0)), pl.BlockSpec((1, d), lambda i: (0, 0))],
        out_specs=pl.BlockSpec((TB, d), lambda i: (i, 0)),
        out_shape=jax.ShapeDtypeStruct((seq, d), F32),
        compiler_params=_cparams("arbitrary"), name="final_norm",
    )(xa, g.reshape(1, d))


def _rope_tables(seq, t, rot_dim, lane_layout):
    quarter = rot_dim // 4
    rows = seq // GRID_W
    inv_freq = ROPE_THETA ** (-jnp.arange(quarter, dtype=F32) / quarter)
    ang_r = jnp.arange(rows, dtype=F32)[:, None] * inv_freq
    ang_c = jnp.arange(GRID_W, dtype=F32)[:, None] * inv_freq

    def table(fn, fill):
        r = jnp.broadcast_to(fn(ang_r)[:, None, :], (rows, GRID_W, quarter))
        c = jnp.broadcast_to(fn(ang_c)[None, :, :], (rows, GRID_W, quarter))
        blk = jnp.concatenate([r, r, c, c], axis=2)
        parts, prev = [], 0
        for lo, hi in lane_layout:
            if lo > prev:
                parts.append(jnp.full((rows, GRID_W, lo - prev), fill, F32))
            parts.append(blk)
            prev = hi
        if prev < LANES:
            parts.append(jnp.full((rows, GRID_W, LANES - prev), fill, F32))
        lat = jnp.concatenate(parts, axis=2).reshape(seq, LANES)
        return jnp.concatenate([lat, jnp.full((t - seq, LANES), fill, F32)], axis=0)

    return table(jnp.cos, 1.0), table(jnp.sin, 0.0)


def _rope_partner_perm(rot_dim):
    q = rot_dim // 4
    src = np.concatenate([np.arange(q, 2 * q), np.arange(0, q), np.arange(3 * q, 4 * q), np.arange(2 * q, 3 * q)])
    sign = np.concatenate([-np.ones(q), np.ones(q), -np.ones(q), np.ones(q)]).astype(np.float32)
    return src, sign


def _swa_mask(ntl):
    out = []
    for i in (0, 1, ntl - 1):
        ts = min(max(i - 1, 0), ntl - 3)
        kpos = ts * TB + np.arange(3 * TB)[:, None]
        qpos = i * TB + np.arange(TB)[None, :]
        out.append(np.where(np.abs(kpos - qpos) <= SWA_WINDOW, 0.0, NEG).astype(np.float32))
    return jnp.asarray(np.stack(out)[:, None])


def _na_bias(rpb, rows, ntl):
    nh = rpb.shape[0]
    rpt = TB // GRID_W
    kr = min(NA_WIN_ROWS, rows)
    c = np.arange(GRID_W)
    cs = np.clip(c - NA_WIN_COLS // 2, 0, GRID_W - NA_WIN_COLS)
    okc = (c[:, None] >= cs[None, :]) & (c[:, None] < cs[None, :] + NA_WIN_COLS)
    dc = np.clip(c[:, None] - c[None, :] + NA_WIN_COLS - 1, 0, 2 * NA_WIN_COLS - 2)
    toep = jnp.where(okc[None, None], rpb[:, :, dc] * LOG2E, NEG)
    neg = jnp.full((nh, GRID_W, GRID_W), NEG, F32)
    out = []
    for i in (0, 1, ntl - 1):
        ts = min(max(i - 1, 0), ntl - 3)
        key_rows = []
        for krow in range(3 * rpt):
            r2 = ts * rpt + krow
            blocks = []
            for qrow in range(rpt):
                r = i * rpt + qrow
                rs = min(max(r - kr // 2, 0), rows - kr)
                blocks.append(toep[:, r2 - r + NA_WIN_ROWS - 1] if rs <= r2 < rs + kr else neg)
            key_rows.append(jnp.concatenate(blocks, axis=2))
        out.append(jnp.concatenate(key_rows, axis=1))
    return jnp.stack(out, axis=0).astype(F32)


def _mixer_na(xa, g, mods, w_qkv, w_o, rpb, ntl, seq):
    d = xa.shape[1]
    hd = d // NA_HEADS
    n = NA_HEADS * hd
    scale = hd ** -0.5 * LOG2E
    colscale = jnp.concatenate([jnp.full((n,), scale, F32), jnp.ones((2 * n,), F32)])
    qt, k, vt = _proj(xa, w_qkv.astype(BF16), ntl=ntl, mods=mods, norm_g=g, mod_idx=(0, 1),
                      colscale=colscale, splits=[(0, n), (n, 2 * n), (2 * n, 3 * n)],
                      out_t=[True, False, True], name="na_qkv")
    bias = _na_bias(rpb, seq // GRID_W, ntl)
    o = _attn("na", qt, k, vt, dk=hd, dv=hd, ntl=ntl, bias=bias)
    (xa,) = _proj(o, w_o.astype(BF16), ntl=ntl, mods=mods, resid=xa, gate_idx=2,
                  out_dtypes=[F32], name="na_out")
    return xa


def _mixer_swa(xa, g, mods, w_qkv, w_o, sink, ntl, seq):
    t, d = xa.shape
    hd = d // SWA_Q_HEADS
    nq, nkv = SWA_Q_HEADS * hd, SWA_KV_HEADS * hd
    scale = hd ** -0.5 * LOG2E
    nrope = nq + nkv
    src, sign = _rope_partner_perm(hd)
    nblk = nrope // hd
    src_full = (np.arange(nblk)[:, None] * hd + src[None, :]).reshape(-1)
    sign_full = np.tile(sign, nblk)
    w2 = w_qkv[:, src_full] * sign_full[None, :]
    cos, sin = _rope_tables(seq, t, hd, [(0, hd), (hd, 2 * hd)])
    colscale = jnp.concatenate([jnp.full((nq,), scale, F32), jnp.ones((2 * nkv,), F32)])
    qt, k, vt = _proj(xa, w_qkv.astype(BF16), ntl=ntl, mods=mods, norm_g=g, mod_idx=(0, 1),
                      w2=w2.astype(BF16), cos=cos, sin=sin, colscale=colscale,
                      splits=[(0, nq), (nq, nrope), (nrope, nrope + nkv)],
                      out_t=[True, False, True], name="swa_qkv")
    k = jnp.transpose(k.reshape(t, SWA_KV_HEADS, hd), (1, 0, 2))
    o = _attn("swa", qt, k, vt, dk=hd, dv=hd, ntl=ntl, bias=_swa_mask(ntl), sink=sink.astype(F32) * LOG2E)
    (xa,) = _proj(o, w_o.astype(BF16), ntl=ntl, mods=mods, resid=xa, gate_idx=2,
                  out_dtypes=[F32], name="swa_out")
    return xa


def _mixer_mla(xa, g, mods, w_dq, q_norm, w_uq, w_dkv, kv_norm, w_ukv, w_o, ntl, seq):
    t, d = xa.shape
    nh = MLA_HEADS
    qr = w_dq.shape[1]
    kvr = kv_norm.shape[0]
    nope, rope, dv = MLA_NOPE_DIM, MLA_ROPE_DIM, MLA_V_DIM
    qk = nope + rope
    pad = LANES - qk
    scale = qk ** -0.5 * LOG2E
    kin = kvr + LANES
    w_dn = jnp.concatenate([w_dq, w_dkv, jnp.zeros((d, LANES - rope), F32)], axis=1)
    g_dn = jnp.concatenate([q_norm, kv_norm, jnp.ones((LANES,), F32)])
    cqn, kvin = _proj(xa, w_dn.astype(BF16), ntl=ntl, mods=mods, norm_g=g, mod_idx=(0, 1),
                      rms_segs=[(0, qr), (qr, qr + kvr)], rms_g=g_dn,
                      splits=[(0, qr), (qr, qr + kin)], name="mla_down")
    src, sign = _rope_partner_perm(rope)
    src_head = np.concatenate([np.arange(nope), nope + src, np.arange(qk, LANES)])
    sign_head = np.concatenate([np.zeros(nope, np.float32), sign, np.zeros(pad, np.float32)])
    src_full = (np.arange(nh)[:, None] * LANES + src_head[None, :]).reshape(-1)
    sign_full = np.tile(sign_head, nh)
    cos, sin = _rope_tables(seq, t, rope, [(nope, qk)])
    wq = jnp.pad(w_uq.reshape(qr, nh, qk), ((0, 0), (0, 0), (0, pad))).reshape(qr, nh * LANES)
    wq2 = wq[:, src_full] * sign_full[None, :]
    (qt,) = _proj(cqn, wq.astype(BF16), ntl=ntl, w2=wq2.astype(BF16), cos=cos, sin=sin,
                  colscale=jnp.full((nh * LANES,), scale, F32), out_t=[True], name="mla_q")
    w_ukv3 = w_ukv.reshape(kvr, nh, nope + dv)
    wk_top = jnp.pad(w_ukv3[:, :, :nope], ((0, 0), (0, 0), (0, LANES - nope)))
    eye = jnp.eye(rope, dtype=F32)
    wk_rope = jnp.pad(eye, ((0, LANES - rope), (nope, pad)))
    wk_bot = jnp.broadcast_to(wk_rope[:, None, :], (LANES, nh, LANES))
    wk = jnp.concatenate([wk_top, wk_bot], axis=0).reshape(kin, nh * LANES)
    wk2 = wk[:, src_full] * sign_full[None, :]
    wv = jnp.concatenate([w_ukv3[:, :, nope:].reshape(kvr, nh * dv), jnp.zeros((LANES, nh * dv), F32)], axis=0)
    (k,) = _proj(kvin, wk.astype(BF16), ntl=ntl, w2=wk2.astype(BF16), cos=cos, sin=sin, name="mla_k")
    (vt,) = _proj(kvin, wv.astype(BF16), ntl=ntl, out_t=[True], name="mla_v")
    o = _attn("mla", qt, k, vt, dk=LANES, dv=dv, ntl=ntl)
    (xa,) = _proj(o, w_o.astype(BF16), ntl=ntl, mods=mods, resid=xa, gate_idx=2,
                  out_dtypes=[F32], name="mla_out")
    return xa


def _moe(xa, g, mods, router, w_gate, w_up, w_down, layer, ntl, seq, with_ctx):
    t, d = xa.shape
    ne = router.shape[1]
    nctx = t - seq
    h, aff = _router(xa, g, mods, router.T, ntl)
    cap_l = EC_CAPACITY_FACTOR * seq // ne
    pos_l, gate_l, off_l = _topk(aff[:, :seq].reshape(ne, seq // CHUNK, CHUNK), cap_l, 0)
    per_tile = TB // CHUNK
    pos = [pos_l.reshape(ne, seq)]
    gate = [gate_l.reshape(ne, seq)]
    offs = [off_l[:, ::per_tile, 0]]
    if with_ctx:
        cap_c = EC_CAPACITY_FACTOR * nctx // ne
        cpad = 8 * CHUNK
        aff_c = jnp.concatenate([aff[:, seq:], jnp.full((ne, cpad - nctx), -1.0, F32)], axis=1)
        pos_c, gate_c, off_c = _topk(aff_c.reshape(ne, 8, CHUNK), cap_c, cap_l)
        pos.append(pos_c.reshape(ne, cpad)[:, :nctx])
        gate.append(gate_c.reshape(ne, cpad)[:, :nctx])
        offs.append(off_c[:, 0:nctx // CHUNK:per_tile, 0])
        nslots = cap_l + cap_c
    else:
        pos.append(jnp.full((ne, nctx), -1.0, F32))
        gate.append(jnp.zeros((ne, nctx), F32))
        offs.append(jnp.full((ne, nctx // TB), cap_l, jnp.int32))
        nslots = cap_l
    pos = jnp.concatenate(pos, axis=1)
    gate = jnp.concatenate(gate, axis=1)
    offs = jnp.concatenate(offs + [jnp.full((ne, 1), nslots, jnp.int32)], axis=1)
    cnts = offs[:, 1:] - offs[:, :-1]
    ys = _ffn(offs, pos, gate, h, w_gate, w_up, w_down, layer=layer, nslots=nslots)
    return _combine(offs, cnts, pos.T, xa, mods, ys, ntl)


def kernel(x, c, ctx, c_ctx, ada_w, ada_b, norm_mix, norm_ffn, na_w_qkv, na_w_o, na_rpb, swa_w_qkv, swa_w_o, swa_sink, mla_w_dq, mla_q_norm, mla_w_uq, mla_w_dkv, mla_kv_norm, mla_w_ukv, mla_w_o, moe_router, moe_w_gate, moe_w_up, moe_w_down, final_norm):
    assert x.shape[0] == 1 and c.shape[0] == 1 and ctx.shape[0] == 1
    seq, d = x.shape[1], x.shape[2]
    nctx = ctx.shape[1]
    assert seq % TB == 0 and nctx == TB and seq // TB >= 4
    depth = ada_w.shape[0]
    ntl = seq // TB
    xa = jnp.concatenate([x[0], ctx[0]], axis=0)
    cs = jnp.concatenate([c, c_ctx[None, :], jnp.zeros((6, d), F32)], axis=0)
    ada = _ada(cs, ada_w, ada_b)
    mods_all = jnp.transpose(ada[:, :2].reshape(depth, 2, 6, d), (0, 2, 1, 3))[:, :, :, None, :]
    for i in range(depth):
        mods = mods_all[i]
        kind, slot = i % N_MIXERS, i // N_MIXERS
        if kind == 0:
            xa = _mixer_na(xa, norm_mix[i], mods, na_w_qkv[slot], na_w_o[slot], na_rpb[slot], ntl, seq)
        elif kind == 1:
            xa = _mixer_swa(xa, norm_mix[i], mods, swa_w_qkv[slot], swa_w_o[slot], swa_sink[slot], ntl, seq)
        else:
            xa = _mixer_mla(xa, norm_mix[i], mods, mla_w_dq[slot], mla_q_norm[slot], mla_w_uq[slot],
                            mla_w_dkv[slot], mla_kv_norm[slot], mla_w_ukv[slot], mla_w_o[slot], ntl, seq)
        xa = _moe(xa, norm_ffn[i], mods, moe_router[i], moe_w_gate, moe_w_up, moe_w_down, i,
                  ntl, seq, with_ctx=i < depth - 1)
    return _final_norm(xa, final_norm, seq)[None]
```

```python
import functools
import math

import jax
import jax.numpy as jnp
import numpy as np
from jax import lax
from jax.experimental import pallas as pl
from jax.experimental.pallas import tpu as pltpu

F32 = jnp.float32
BF16 = jnp.bfloat16
HIGHEST = lax.Precision.HIGHEST

GRID_W = 64
N_MIXERS = 3
RMS_EPS = 1e-6
ROPE_THETA = 10000.0
NA_HEADS = 16
NA_WIN_ROWS = 8
NA_WIN_COLS = 16
SWA_Q_HEADS = 16
SWA_KV_HEADS = 4
SWA_WINDOW = 128
MLA_HEADS = 16
MLA_NOPE_DIM = 64
MLA_ROPE_DIM = 32
MLA_V_DIM = 64
N_EXPERTS = 16
EC_CAPACITY_FACTOR = 2

TB = 256
CHUNK = 128
LANES = 128
NEG = -1e30
LOG2E = math.log2(math.e)
FFN_ROWS = 416
FFN_FCHUNK = 512
GATHER_WIN = TB + 16
COMBINE_WIN = 64
VMEM_LIMIT = 56 * 1024 * 1024


def _cparams(*sem):
    return pltpu.CompilerParams(dimension_semantics=sem, vmem_limit_bytes=VMEM_LIMIT)


def _ada_kernel(cs_ref, w_ref, b_ref, o_ref):
    x = cs_ref[...]
    x = x * jax.nn.sigmoid(x)
    y = jnp.dot(x, w_ref[0], precision=HIGHEST, preferred_element_type=F32)
    o_ref[0] = y + b_ref[0]


def _ada(cs, ada_w, ada_b):
    depth, d, n = ada_w.shape
    nb = n // d
    return pl.pallas_call(
        _ada_kernel,
        grid=(depth, nb),
        in_specs=[
            pl.BlockSpec((8, d), lambda i, k: (0, 0)),
            pl.BlockSpec((1, d, d), lambda i, k: (i, 0, k)),
            pl.BlockSpec((1, 1, d), lambda i, k: (i, 0, k)),
        ],
        out_specs=pl.BlockSpec((1, 8, d), lambda i, k: (i, 0, k)),
        out_shape=jax.ShapeDtypeStruct((depth, 8, n), F32),
        compiler_params=_cparams("arbitrary", "arbitrary"),
        name="ada",
    )(cs, ada_w, ada_b.reshape(depth, 1, n))


def _normmod(x, g, shift, scale):
    y = x * lax.rsqrt(jnp.mean(x * x, axis=-1, keepdims=True) + RMS_EPS)
    return (y * g) * (1.0 + scale) + shift


def _proj_kernel(*refs, normmod, rms_segs, rope_cols, colscale, resid, splits, out_t):
    it = iter(refs)
    x_ref = next(it)
    if normmod:
        g_ref, sh_ref, sc_ref = next(it), next(it), next(it)
    w_ref = next(it)
    if rms_segs:
        g2_ref = next(it)
    if rope_cols:
        w2_ref, cos_ref, sin_ref = next(it), next(it), next(it)
    if colscale:
        cs_ref = next(it)
    if resid:
        res_ref, gate_ref = next(it), next(it)
    out_refs = list(it)

    x = x_ref[...]
    if normmod:
        x = _normmod(x.astype(F32), g_ref[...], sh_ref[0, 0], sc_ref[0, 0])
    xb = x.astype(BF16)
    y = jnp.dot(xb, w_ref[...], preferred_element_type=F32)
    if rms_segs:
        parts = []
        prev = 0
        for (a, b) in rms_segs:
            if a > prev:
                parts.append(y[:, prev:a])
            seg = y[:, a:b]
            seg = seg * lax.rsqrt(jnp.mean(seg * seg, axis=-1, keepdims=True) + RMS_EPS)
            parts.append(seg * g2_ref[:, a:b])
            prev = b
        if prev < y.shape[1]:
            parts.append(y[:, prev:])
        y = jnp.concatenate(parts, axis=1)
    if rope_cols:
        y2 = jnp.dot(xb, w2_ref[...], preferred_element_type=F32)
        reps = rope_cols // LANES
        cos = jnp.tile(cos_ref[...], (1, reps))
        sin = jnp.tile(sin_ref[...], (1, reps))
        yr = y[:, :rope_cols] * cos + y2 * sin
        y = yr if rope_cols == y.shape[1] else jnp.concatenate([yr, y[:, rope_cols:]], axis=1)
    if colscale:
        y = y * cs_ref[...]
    if resid:
        y = res_ref[...] + gate_ref[0, 0] * y
    for o_ref, (a, b), tr in zip(out_refs, splits, out_t):
        if tr:
            o_ref[0] = y[:, a:b].T.astype(o_ref.dtype)
        else:
            o_ref[...] = y[:, a:b].astype(o_ref.dtype)


def _proj(x, w, *, ntl, mods=None, norm_g=None, mod_idx=None, rms_segs=None, rms_g=None,
          w2=None, cos=None, sin=None, colscale=None, resid=None, gate_idx=None,
          splits=None, out_dtypes=None, out_t=None, name="proj"):
    t, kdim = x.shape
    n = w.shape[1]
    nt = t // TB
    splits = splits or [(0, n)]
    out_dtypes = out_dtypes or [BF16] * len(splits)

    def modspec(k):
        dm = mods.shape[-1]
        return pl.BlockSpec((1, 1, 1, dm), lambda i: (k, i // ntl, 0, 0))

    args, specs = [x], [pl.BlockSpec((TB, kdim), lambda i: (i, 0))]
    if norm_g is not None:
        args += [norm_g.reshape(1, kdim), mods, mods]
        specs += [pl.BlockSpec((1, kdim), lambda i: (0, 0)), modspec(mod_idx[0]), modspec(mod_idx[1])]
    args.append(w)
    specs.append(pl.BlockSpec((kdim, n), lambda i: (0, 0)))
    if rms_segs:
        args.append(rms_g.reshape(1, n))
        specs.append(pl.BlockSpec((1, n), lambda i: (0, 0)))
    rope_cols = 0
    if w2 is not None:
        rope_cols = w2.shape[1]
        args += [w2, cos, sin]
        specs += [pl.BlockSpec((kdim, rope_cols), lambda i: (0, 0)),
                  pl.BlockSpec((TB, LANES), lambda i: (i, 0)),
                  pl.BlockSpec((TB, LANES), lambda i: (i, 0))]
    if colscale is not None:
        args.append(colscale.reshape(1, n))
        specs.append(pl.BlockSpec((1, n), lambda i: (0, 0)))
    if resid is not None:
        args += [resid, mods]
        specs += [pl.BlockSpec((TB, n), lambda i: (i, 0)), modspec(gate_idx)]
    out_t = out_t or [False] * len(splits)
    out_shape = [jax.ShapeDtypeStruct((nt, b - a, TB) if tr else (t, b - a), dt)
                 for (a, b), dt, tr in zip(splits, out_dtypes, out_t)]
    out_specs = [pl.BlockSpec((1, b - a, TB), lambda i: (i, 0, 0)) if tr
                 else pl.BlockSpec((TB, b - a), lambda i: (i, 0)) for (a, b), tr in zip(splits, out_t)]
    kern = functools.partial(
        _proj_kernel, normmod=norm_g is not None, rms_segs=rms_segs, rope_cols=rope_cols,
        colscale=colscale is not None, resid=resid is not None, splits=splits, out_t=out_t)
    outs = pl.pallas_call(
        kern, grid=(nt,), in_specs=specs, out_specs=out_specs, out_shape=out_shape,
        compiler_params=_cparams("arbitrary"), name=name,
    )(*args)
    return outs


ONES_ROWS = 16


def _softmax_part(st, m):
    m_new = jnp.maximum(m, jnp.max(st, axis=0, keepdims=True))
    return m_new, jnp.exp2(m - m_new), jnp.exp2(st - m_new).astype(BF16)


def _pv_part(vt, p, alpha, acc):
    return alpha * acc + jnp.dot(vt, p, preferred_element_type=F32)


def _softmax_step(st, vt, carry):
    m, acc = carry
    m, alpha, p = _softmax_part(st, m)
    return m, _pv_part(vt, p, alpha, acc)


def _softmax_init(dv):
    return (jnp.full((1, TB), NEG, F32), jnp.zeros((dv + ONES_ROWS, TB), F32))


def _softmax_finish(carry, dv, sink=None):
    m, acc = carry
    num, l = acc[:dv], acc[dv:dv + 1]
    if sink is not None:
        m_f = jnp.maximum(m, sink)
        a = jnp.exp2(m - m_f)
        l = l * a + jnp.exp2(sink - m_f)
        num = num * a
    return (num / l).T


def _scores_t(k, qt):
    return jnp.dot(k, qt, preferred_element_type=F32)


def _attn_kernel(*refs, kind, dk, dv, ntl, hps, nq):
    refs = list(refs)
    sink_ref = refs.pop(0) if kind == "swa" else None
    q_ref, k_ref, vt_ref = refs[:3]
    nbias = nq if kind != "mla" else 0
    bias_refs = refs[3:3 + nbias]
    o_ref = refs[3 + nbias]
    sa_ref, sb_ref, pa_ref, pb_ref = refs[4 + nbias:]
    hp = pl.program_id(0)
    i = pl.program_id(1)
    shared_kv = kind == "swa"
    chains = [(qi, hh) for qi in range(nq) for hh in range(hps)]
    nchain = len(chains)
    qs = [q_ref[qi, hh * dk:(hh + 1) * dk, :] for qi, hh in chains]
    ones = jnp.ones((ONES_ROWS, TB), BF16)

    def scores(c, kt):
        start = pl.multiple_of(kt * TB, TB)
        hh = chains[c][1]
        if shared_kv:
            return _scores_t(k_ref[0, pl.ds(start, TB), :], qs[c])
        return _scores_t(k_ref[pl.ds(start, TB), hh * dk:(hh + 1) * dk], qs[c])

    def vt_tile(c, kt):
        lo = 0 if shared_kv else chains[c][1] * dv
        return jnp.concatenate([vt_ref[kt, lo:lo + dv, :], ones], axis=0)

    states = tuple((m, jnp.ones((1, TB), F32), acc) for m, acc in (_softmax_init(dv) for _ in chains))

    def stage(score_next, kt_prev, s_cur, s_nxt, p_cur, p_prev, states):
        if score_next is not None:
            for c in range(nchain):
                s_nxt[c] = score_next(c)
        new = []
        for c in range(nchain):
            m, alpha, acc = states[c]
            if p_prev is not None:
                acc = _pv_part(vt_tile(c, kt_prev(c)), p_prev[c], alpha, acc)
            m, alpha, p = _softmax_part(s_cur[c], m)
            p_cur[c] = p
            new.append((m, alpha, acc))
        return tuple(new)

    if kind == "mla":
        def latent(kt):
            return lambda c: scores(c, jnp.minimum(kt, ntl))

        def body(j, states):
            states = stage(latent(2 * j + 2), lambda c: 2 * j, sb_ref, sa_ref, pb_ref, pa_ref, states)
            return stage(latent(2 * j + 3), lambda c: 2 * j + 1, sa_ref, sb_ref, pa_ref, pb_ref, states)

        for c in range(nchain):
            sa_ref[c] = scores(c, 0)
        states = stage(latent(1), None, sa_ref, sb_ref, pa_ref, None, states)
        states = lax.fori_loop(0, ntl // 2, body, states, unroll=8 if ntl % 16 == 0 else 1)

        def last_tile(c):
            return ntl
    else:
        tiles = []
        for qi in range(nq):
            ts = jnp.clip(i * nq + qi - 1, 0, ntl - 3)
            tiles.append([ts, ts + 1, ts + 2, ntl])

        def masked(s):
            def fn(c):
                qi, hh = chains[c]
                st = scores(c, tiles[qi][s])
                if s == 3:
                    return st
                return st + bias_refs[qi][0, hh if kind == "na" else 0, s * TB:(s + 1) * TB, :]
            return fn

        def tile_of(s):
            return lambda c: tiles[chains[c][0]][s]

        for c in range(nchain):
            sa_ref[c] = masked(0)(c)
        states = stage(masked(1), None, sa_ref, sb_ref, pa_ref, None, states)
        states = stage(masked(2), tile_of(0), sb_ref, sa_ref, pb_ref, pa_ref, states)
        states = stage(masked(3), tile_of(1), sa_ref, sb_ref, pa_ref, pb_ref, states)
        states = stage(None, tile_of(2), sb_ref, sa_ref, pb_ref, pa_ref, states)
        last_tile = tile_of(3)
    last_p = pa_ref if kind == "mla" else pb_ref
    for qi in range(nq):
        outs = []
        for hh in range(hps):
            c = qi * hps + hh
            m, alpha, acc = states[c]
            acc = _pv_part(vt_tile(c, last_tile(c)), last_p[c], alpha, acc)
            sink = sink_ref[hps * hp + hh] if kind == "swa" else None
            outs.append(_softmax_finish((m, acc), dv, sink))
        o_ref[qi * TB:(qi + 1) * TB, :] = jnp.concatenate(outs, axis=1).astype(o_ref.dtype)


def _attn_ctx_kernel(*refs, dk, dv, hps, has_sink, shared_kv):
    if has_sink:
        sink_ref, q_ref, k_ref, vt_ref, _, o_ref = refs
    else:
        q_ref, k_ref, vt_ref, _, o_ref = refs
    hp = pl.program_id(0)
    ones = jnp.ones((ONES_ROWS, TB), BF16)
    outs = []
    for hh in range(hps):
        k = k_ref[0] if shared_kv else k_ref[:, hh * dk:(hh + 1) * dk]
        lo = 0 if shared_kv else hh * dv
        st = _scores_t(k, q_ref[0, hh * dk:(hh + 1) * dk, :])
        vt = jnp.concatenate([vt_ref[0, lo:lo + dv, :], ones], axis=0)
        carry = _softmax_step(st, vt, _softmax_init(dv))
        outs.append(_softmax_finish(carry, dv, sink_ref[hps * hp + hh] if has_sink else None))
    o_ref[...] = jnp.concatenate(outs, axis=1).astype(o_ref.dtype)


def _attn(kind, qt, k, vt, *, dk, dv, ntl, bias=None, sink=None):
    nt, _, _ = qt.shape
    t = nt * TB
    nh = qt.shape[1] // dk
    shared_kv = kind == "swa"
    hps = 2 if kind == "mla" else 4
    kvs = 1 if shared_kv else hps
    nq = 1 if kind == "mla" else 2
    assert not shared_kv or nh // k.shape[0] == hps
    assert ntl % 2 == 0
    kern = functools.partial(_attn_kernel, kind=kind, dk=dk, dv=dv, ntl=ntl, hps=hps, nq=nq)
    local = kind != "mla"
    resident = dict(pipeline_mode=pl.Buffered(1)) if local else {}
    in_specs = [
        pl.BlockSpec((nq, hps * dk, TB), lambda hp, i, *_: (i, hp, 0)),
        pl.BlockSpec((1, t, dk), lambda hp, i, *_: (hp, 0, 0), **resident) if shared_kv
        else pl.BlockSpec((t, hps * dk), lambda hp, i, *_: (0, hp), **resident),
        pl.BlockSpec((nt, kvs * dv, TB), lambda hp, i, *_: (0, hp, 0), **resident),
    ]
    args = [qt, k, vt]
    nsp = 0
    if local:
        per_head = kind == "na"

        def bias_spec(qi):
            def bias_map(hp, i, *_):
                tile = i * nq + qi
                pat = jnp.where(tile == 0, 0, jnp.where(tile >= ntl - 1, 2, 1))
                return (pat, hp if per_head else 0, 0, 0)
            return pl.BlockSpec((1, hps if per_head else 1, 3 * TB, TB), bias_map)
        in_specs += [bias_spec(qi) for qi in range(nq)]
        args += [bias] * nq
    if kind == "swa":
        nsp = 1
        args = [sink] + args
    nchain = nq * hps
    o = pl.pallas_call(
        kern,
        grid_spec=pltpu.PrefetchScalarGridSpec(
            num_scalar_prefetch=nsp, grid=(nh // hps, ntl // nq), in_specs=in_specs,
            out_specs=pl.BlockSpec((nq * TB, hps * dv), lambda hp, i, *_: (i, hp)),
            scratch_shapes=[pltpu.VMEM((nchain, TB, TB), F32)] * 2 + [pltpu.VMEM((nchain, TB, TB), BF16)] * 2),
        out_shape=jax.ShapeDtypeStruct((t, nh * dv), BF16),
        compiler_params=_cparams("arbitrary", "arbitrary"), name="attn_" + kind,
    )(*args)
    has_sink = kind == "swa"
    ckern = functools.partial(_attn_ctx_kernel, dk=dk, dv=dv, hps=hps, has_sink=has_sink,
                              shared_kv=shared_kv)
    cargs = ([sink] if has_sink else []) + [qt, k, vt, o]
    return pl.pallas_call(
        ckern,
        grid_spec=pltpu.PrefetchScalarGridSpec(
            num_scalar_prefetch=nsp, grid=(nh // hps,),
            in_specs=[pl.BlockSpec((1, hps * dk, TB), lambda hp, *_: (ntl, hp, 0)),
                      pl.BlockSpec((1, TB, dk), lambda hp, *_: (hp, ntl, 0)) if shared_kv
                      else pl.BlockSpec((TB, hps * dk), lambda hp, *_: (ntl, hp)),
                      pl.BlockSpec((1, kvs * dv, TB), lambda hp, *_: (ntl, hp, 0)),
                      pl.BlockSpec(memory_space=pl.ANY)],
            out_specs=pl.BlockSpec((TB, hps * dv), lambda hp, *_: (ntl, hp))),
        out_shape=jax.ShapeDtypeStruct((t, nh * dv), BF16),
        input_output_aliases={len(cargs) - 1: 0},
        compiler_params=_cparams("arbitrary"), name="attn_ctx_" + kind,
    )(*cargs)


def _router_kernel(x_ref, g_ref, sh_ref, sc_ref, rt_ref, h_ref, aff_ref):
    h = _normmod(x_ref[...], g_ref[...], sh_ref[0, 0], sc_ref[0, 0])
    h_ref[...] = h.astype(BF16)
    lg = lax.dot_general(rt_ref[...], h, (((1,), (1,)), ((), ())),
                         precision=HIGHEST, preferred_element_type=F32)
    lg = lg - jnp.max(lg, axis=0, keepdims=True)
    e = jnp.exp(lg)
    aff_ref[...] = e / jnp.sum(e, axis=0, keepdims=True)


def _router(xa, g, mods, router_t, ntl):
    t, d = xa.shape
    ne = router_t.shape[0]

    def modspec(k):
        return pl.BlockSpec((1, 1, 1, d), lambda i: (k, i // ntl, 0, 0))

    return pl.pallas_call(
        _router_kernel, grid=(t // TB,),
        in_specs=[pl.BlockSpec((TB, d), lambda i: (i, 0)),
                  pl.BlockSpec((1, d), lambda i: (0, 0)),
                  modspec(3), modspec(4),
                  pl.BlockSpec((ne, d), lambda i: (0, 0))],
        out_specs=[pl.BlockSpec((TB, d), lambda i: (i, 0)),
                   pl.BlockSpec((ne, TB), lambda i: (0, i))],
        out_shape=[jax.ShapeDtypeStruct((t, d), BF16), jax.ShapeDtypeStruct((ne, t), F32)],
        compiler_params=_cparams("arbitrary"), name="router",
    )(xa, g.reshape(1, d), mods, mods, router_t)


def _topk_kernel(aff_ref, pos_ref, gate_ref, off_ref, *, cap, base):
    x = aff_ref[...]
    ne, nc, _ = x.shape
    bits = lax.bitcast_convert_type(x, jnp.int32)

    def count(mask):
        c = jnp.sum(jnp.where(mask, 1.0, 0.0), axis=1, keepdims=True)
        return jnp.sum(c, axis=2, keepdims=True)

    thr = jnp.zeros((ne, 1, 1), jnp.int32)
    for b in range(30, -1, -1):
        cand = thr | (1 << b)
        thr = jnp.where(count(bits >= cand) >= cap, cand, thr)

    ia = lax.broadcasted_iota(jnp.int32, (LANES, LANES), 0)
    ib = lax.broadcasted_iota(jnp.int32, (LANES, LANES), 1)
    upper = jnp.where(ia <= ib, 1.0, 0.0)
    ones = jnp.ones((LANES, LANES), F32)
    ca = lax.broadcasted_iota(jnp.int32, (nc, nc), 0)
    cb = lax.broadcasted_iota(jnp.int32, (nc, nc), 1)
    lower = jnp.where(cb < ca, 1.0, 0.0)

    def prefix(mask):
        m2 = jnp.where(mask, 1.0, 0.0).reshape(ne * nc, LANES)
        within = jnp.dot(m2, upper, precision=HIGHEST, preferred_element_type=F32)
        tot = jnp.dot(m2, ones, precision=HIGHEST, preferred_element_type=F32).reshape(ne, nc, LANES)
        offs = [jnp.dot(lower, tot[e], precision=HIGHEST, preferred_element_type=F32)[None]
                for e in range(ne)]
        off = jnp.concatenate(offs, axis=0)
        return within.reshape(ne, nc, LANES) + off, off

    gt = bits > thr
    eq = bits == thr
    need = cap - count(gt)
    eq_rank, _ = prefix(eq)
    sel = gt | (eq & (eq_rank <= need))
    sel_rank, off = prefix(sel)
    pos_ref[...] = jnp.where(sel, sel_rank - 1.0 + base, -1.0)
    gate_ref[...] = jnp.where(sel, x, 0.0)
    off_ref[...] = off.astype(jnp.int32) + base


def _topk(aff3, cap, base):
    ne, nc, _ = aff3.shape
    kern = functools.partial(_topk_kernel, cap=cap, base=base)
    spec = pl.BlockSpec((ne, nc, LANES), lambda i: (0, 0, 0))
    return pl.pallas_call(
        kern, grid=(1,), in_specs=[spec], out_specs=[spec, spec, spec],
        out_shape=[jax.ShapeDtypeStruct(aff3.shape, F32), jax.ShapeDtypeStruct(aff3.shape, F32),
                   jax.ShapeDtypeStruct(aff3.shape, jnp.int32)],
        compiler_params=_cparams("arbitrary"), name="topk",
    )(aff3)


def _ffn_kernel(offs_ref, pos_ref, gate_ref, h_ref, wg_ref, wu_ref, wd_ref, y_ref,
                xs_ref, xb_ref, gs_ref, *, ng, gt, nf, nrows):
    e = pl.program_id(0)
    j = pl.program_id(1)

    @pl.when(j == 0)
    def _():
        xs_ref[...] = jnp.zeros_like(xs_ref)
        gs_ref[...] = jnp.zeros_like(gs_ref)

    @pl.when(j < ng)
    def _():
        for s in range(gt):
            off = offs_ref[e, j * gt + s]
            w0 = pl.multiple_of((off // 8) * 8, 8)
            prow = pos_ref[0, :, s * TB:(s + 1) * TB]
            grow = gate_ref[0, :, s * TB:(s + 1) * TB]
            ids = (w0 + lax.broadcasted_iota(jnp.int32, (GATHER_WIN, 1), 0)).astype(F32)
            hit = prow == ids
            onehot = jnp.where(hit, 1.0, 0.0).astype(BF16)
            xs_ref[pl.ds(w0, GATHER_WIN), :] += jnp.dot(
                onehot, h_ref[s * TB:(s + 1) * TB, :], preferred_element_type=F32)
            gsel = jnp.sum(jnp.where(hit, grow, 0.0), axis=1, keepdims=True)
            gs_ref[pl.ds(w0, GATHER_WIN), :] += jnp.broadcast_to(gsel, (GATHER_WIN, LANES))

    @pl.when(j == ng)
    def _():
        xb_ref[...] = xs_ref[0:nrows, :].astype(BF16)
        xs_ref[...] = jnp.zeros_like(xs_ref)

    @pl.when(j >= ng)
    def _():
        wg = wg_ref[0, 0].astype(BF16)
        wu = wu_ref[0, 0].astype(BF16)
        wd = wd_ref[0, 0].astype(BF16)
        for b in range(nrows // FFN_ROWS):
            rows = slice(b * FFN_ROWS, (b + 1) * FFN_ROWS)
            x = xb_ref[rows, :]
            a = jnp.dot(x, wg, preferred_element_type=F32)
            u = jnp.dot(x, wu, preferred_element_type=F32)
            hmid = (a * jax.nn.sigmoid(a) * u).astype(BF16)
            xs_ref[rows, :] += jnp.dot(hmid, wd, preferred_element_type=F32)

    @pl.when(j == ng + nf - 1)
    def _():
        y_ref[0, 0:nrows, :] = (xs_ref[0:nrows, :] * gs_ref[0:nrows, 0:1]).astype(y_ref.dtype)
        if y_ref.shape[1] > nrows:
            y_ref[0, nrows:, :] = jnp.zeros((y_ref.shape[1] - nrows, y_ref.shape[2]), y_ref.dtype)


def _ffn(offs, pos, gate, h, wg, wu, wd, *, layer, nslots):
    t, d = h.shape
    nt = t // TB
    _, ne, _, f = wg.shape
    gt = max(g for g in range(1, 9) if nt % g == 0)
    ng = nt // gt
    nf = f // FFN_FCHUNK
    nrows = -(-nslots // FFN_ROWS) * FFN_ROWS
    srows = -(-(nslots + TB) // 16) * 16
    srows = max(srows, nrows)
    xs_rows = max(nrows, nslots + GATHER_WIN)
    kern = functools.partial(_ffn_kernel, ng=ng, gt=gt, nf=nf, nrows=nrows)

    def fchunk(j):
        return jnp.clip(j - ng, 0, nf - 1)

    grid_spec = pltpu.PrefetchScalarGridSpec(
        num_scalar_prefetch=1, grid=(ne, ng + nf),
        in_specs=[
            pl.BlockSpec((1, 1, gt * TB), lambda e, j, o: (e, 0, jnp.minimum(j, ng - 1))),
            pl.BlockSpec((1, 1, gt * TB), lambda e, j, o: (e, 0, jnp.minimum(j, ng - 1))),
            pl.BlockSpec((gt * TB, d), lambda e, j, o: (jnp.minimum(j, ng - 1), 0)),
            pl.BlockSpec((1, 1, d, FFN_FCHUNK), lambda e, j, o: (layer, e, 0, fchunk(j))),
            pl.BlockSpec((1, 1, d, FFN_FCHUNK), lambda e, j, o: (layer, e, 0, fchunk(j))),
            pl.BlockSpec((1, 1, FFN_FCHUNK, d), lambda e, j, o: (layer, e, fchunk(j), 0)),
        ],
        out_specs=pl.BlockSpec((1, srows, d), lambda e, j, o: (e, 0, 0)),
        scratch_shapes=[pltpu.VMEM((xs_rows, d), F32), pltpu.VMEM((nrows, d), BF16),
                        pltpu.VMEM((xs_rows, LANES), F32)])
    return pl.pallas_call(
        kern, grid_spec=grid_spec,
        out_shape=jax.ShapeDtypeStruct((ne, srows, d), BF16),
        compiler_params=_cparams("arbitrary", "arbitrary"), name="ffn",
    )(offs, pos.reshape(ne, 1, t), gate.reshape(ne, 1, t), h, wg, wu, wd)


def _combine_kernel(*refs, ne, srows, final):
    offs_ref, cnt_ref = refs[0], refs[1]
    posc_ref, xa_ref, gate_ref = refs[2], refs[3], refs[4]
    win_refs = refs[5:5 + ne]
    win2_refs = refs[5 + ne:5 + 2 * ne]
    fg_ref = refs[5 + 2 * ne] if final else None
    o_ref = refs[5 + 2 * ne + (1 if final else 0)]
    j = pl.program_id(0)
    lane = lax.broadcasted_iota(jnp.int32, (1, LANES), 1)
    first = lane < COMBINE_WIN
    blocks = []
    for e in range(0, ne, 2):
        w0a = (offs_ref[e, j] // 16) * 16
        w0b = (offs_ref[e + 1, j] // 16) * 16
        ids = jnp.where(first, w0a + lane, w0b + lane - COMBINE_WIN).astype(F32)
        pcol = jnp.where(first, posc_ref[:, e:e + 1], posc_ref[:, e + 1:e + 2])
        blocks.append(jnp.where(pcol == ids, 1.0, 0.0).astype(BF16))
    onehot = jnp.concatenate(blocks, axis=1)
    ycat = jnp.concatenate([w[...] for w in win_refs], axis=0)
    acc = jnp.dot(onehot, ycat, preferred_element_type=F32)
    o_ref[...] = xa_ref[...] + gate_ref[0, 0] * acc

    wide = lax.broadcasted_iota(jnp.int32, (1, TB), 1)
    for e in range(ne):
        off = offs_ref[e, j]
        w0 = (off // 16) * 16

        @pl.when(off - w0 + cnt_ref[e, j] > COMBINE_WIN)
        def _(e=e, w0=w0):
            w1 = jnp.minimum(w0 + COMBINE_WIN, srows - TB)
            ids = w1 + wide
            hit = (posc_ref[:, e:e + 1] == ids.astype(F32)) & (ids >= w0 + COMBINE_WIN)
            oh = jnp.where(hit, 1.0, 0.0).astype(BF16)
            o_ref[...] += gate_ref[0, 0] * jnp.dot(oh, win2_refs[e][...], preferred_element_type=F32)

    if final:
        x = o_ref[...]
        o_ref[...] = x * lax.rsqrt(jnp.mean(x * x, axis=-1, keepdims=True) + RMS_EPS) * fg_ref[...]


def _combine(offs, cnts, pos_cols, xa, mods, ys, ntl, final_g=None):
    t, d = xa.shape
    nt = ntl if final_g is not None else t // TB
    ne, srows, _ = ys.shape
    assert srows % 16 == 0
    ys2 = ys.reshape(ne * srows, d)

    assert ne % 2 == 0 and 2 * COMBINE_WIN == LANES

    def win_spec(e):
        def imap(j, offs, cnts):
            return ((e * srows // 16 + offs[e, j] // 16) * 16, 0)
        return pl.BlockSpec((pl.Element(COMBINE_WIN), pl.Element(d)), imap)

    def win2_spec(e):
        def imap(j, offs, cnts):
            off = offs[e, j]
            w0 = (off // 16) * 16
            need = off - w0 + cnts[e, j] > COMBINE_WIN
            w1 = jnp.where(need, jnp.minimum(w0 + COMBINE_WIN, srows - TB), 0)
            return ((e * srows // 16 + w1 // 16) * 16, 0)
        return pl.BlockSpec((pl.Element(TB), pl.Element(d)), imap)

    in_specs = [
        pl.BlockSpec((TB, ne), lambda j, o, c: (j, 0)),
        pl.BlockSpec((TB, d), lambda j, o, c: (j, 0)),
        pl.BlockSpec((1, 1, 1, d), lambda j, o, c: (5, j // ntl, 0, 0)),
    ] + [win_spec(e) for e in range(ne)] + [win2_spec(e) for e in range(ne)]
    args = [offs, cnts, pos_cols, xa, mods] + [ys2] * (2 * ne)
    if final_g is not None:
        in_specs.append(pl.BlockSpec((1, d), lambda j, o, c: (0, 0)))
        args.append(final_g.reshape(1, d))
    grid_spec = pltpu.PrefetchScalarGridSpec(
        num_scalar_prefetch=2, grid=(nt,), in_specs=in_specs,
        out_specs=pl.BlockSpec((TB, d), lambda j, o, c: (j, 0)))
    kern = functools.partial(_combine_kernel, ne=ne, srows=srows, final=final_g is not None)
    return pl.pallas_call(
        kern, grid_spec=grid_spec, out_shape=jax.ShapeDtypeStruct((nt * TB, d), F32),
        compiler_params=_cparams("arbitrary"), name="combine",
    )(*args)


def _rope_tables(seq, t, rot_dim, lane_layout):
    quarter = rot_dim // 4
    rows = seq // GRID_W
    inv_freq = ROPE_THETA ** (-jnp.arange(quarter, dtype=F32) / quarter)
    ang_r = jnp.arange(rows, dtype=F32)[:, None] * inv_freq
    ang_c = jnp.arange(GRID_W, dtype=F32)[:, None] * inv_freq

    def table(fn, fill):
        r = jnp.broadcast_to(fn(ang_r)[:, None, :], (rows, GRID_W, quarter))
        c = jnp.broadcast_to(fn(ang_c)[None, :, :], (rows, GRID_W, quarter))
        blk = jnp.concatenate([r, r, c, c], axis=2)
        parts, prev = [], 0
        for lo, hi in lane_layout:
            if lo > prev:
                parts.append(jnp.full((rows, GRID_W, lo - prev), fill, F32))
            parts.append(blk)
            prev = hi
        if prev < LANES:
            parts.append(jnp.full((rows, GRID_W, LANES - prev), fill, F32))
        lat = jnp.concatenate(parts, axis=2).reshape(seq, LANES)
        return jnp.concatenate([lat, jnp.full((t - seq, LANES), fill, F32)], axis=0)

    return table(jnp.cos, 1.0), table(jnp.sin, 0.0)


def _rope_partner_perm(rot_dim):
    q = rot_dim // 4
    src = np.concatenate([np.arange(q, 2 * q), np.arange(0, q), np.arange(3 * q, 4 * q), np.arange(2 * q, 3 * q)])
    sign = np.concatenate([-np.ones(q), np.ones(q), -np.ones(q), np.ones(q)]).astype(np.float32)
    return src, sign


def _swa_mask(ntl):
    out = []
    for i in (0, 1, ntl - 1):
        ts = min(max(i - 1, 0), ntl - 3)
        kpos = ts * TB + np.arange(3 * TB)[:, None]
        qpos = i * TB + np.arange(TB)[None, :]
        out.append(np.where(np.abs(kpos - qpos) <= SWA_WINDOW, 0.0, NEG).astype(np.float32))
    return jnp.asarray(np.stack(out)[:, None])


def _na_bias(rpb, rows, ntl):
    nh = rpb.shape[0]
    rpt = TB // GRID_W
    kr = min(NA_WIN_ROWS, rows)
    c = np.arange(GRID_W)
    cs = np.clip(c - NA_WIN_COLS // 2, 0, GRID_W - NA_WIN_COLS)
    okc = (c[:, None] >= cs[None, :]) & (c[:, None] < cs[None, :] + NA_WIN_COLS)
    dc = np.clip(c[:, None] - c[None, :] + NA_WIN_COLS - 1, 0, 2 * NA_WIN_COLS - 2)
    toep = jnp.where(okc[None, None], rpb[:, :, dc] * LOG2E, NEG)
    neg = jnp.full((nh, GRID_W, GRID_W), NEG, F32)
    out = []
    for i in (0, 1, ntl - 1):
        ts = min(max(i - 1, 0), ntl - 3)
        key_rows = []
        for krow in range(3 * rpt):
            r2 = ts * rpt + krow
            blocks = []
            for qrow in range(rpt):
                r = i * rpt + qrow
                rs = min(max(r - kr // 2, 0), rows - kr)
                blocks.append(toep[:, r2 - r + NA_WIN_ROWS - 1] if rs <= r2 < rs + kr else neg)
            key_rows.append(jnp.concatenate(blocks, axis=2))
        out.append(jnp.concatenate(key_rows, axis=1))
    return jnp.stack(out, axis=0).astype(F32)


def _mixer_na(xa, g, mods, w_qkv, w_o, rpb, ntl, seq):
    d = xa.shape[1]
    hd = d // NA_HEADS
    n = NA_HEADS * hd
    scale = hd ** -0.5 * LOG2E
    colscale = jnp.concatenate([jnp.full((n,), scale, F32), jnp.ones((2 * n,), F32)])
    qt, k, vt = _proj(xa, w_qkv.astype(BF16), ntl=ntl, mods=mods, norm_g=g, mod_idx=(0, 1),
                      colscale=colscale, splits=[(0, n), (n, 2 * n), (2 * n, 3 * n)],
                      out_t=[True, False, True], name="na_qkv")
    bias = _na_bias(rpb, seq // GRID_W, ntl)
    o = _attn("na", qt, k, vt, dk=hd, dv=hd, ntl=ntl, bias=bias)
    (xa,) = _proj(o, w_o.astype(BF16), ntl=ntl, mods=mods, resid=xa, gate_idx=2,
                  out_dtypes=[F32], name="na_out")
    return xa


def _mixer_swa(xa, g, mods, w_qkv, w_o, sink, ntl, seq):
    t, d = xa.shape
    hd = d // SWA_Q_HEADS
    nq, nkv = SWA_Q_HEADS * hd, SWA_KV_HEADS * hd
    scale = hd ** -0.5 * LOG2E
    nrope = nq + nkv
    src, sign = _rope_partner_perm(hd)
    nblk = nrope // hd
    src_full = (np.arange(nblk)[:, None] * hd + src[None, :]).reshape(-1)
    sign_full = np.tile(sign, nblk)
    w2 = w_qkv[:, src_full] * sign_full[None, :]
    cos, sin = _rope_tables(seq, t, hd, [(0, hd), (hd, 2 * hd)])
    colscale = jnp.concatenate([jnp.full((nq,), scale, F32), jnp.ones((2 * nkv,), F32)])
    qt, k, vt = _proj(xa, w_qkv.astype(BF16), ntl=ntl, mods=mods, norm_g=g, mod_idx=(0, 1),
                      w2=w2.astype(BF16), cos=cos, sin=sin, colscale=colscale,
                      splits=[(0, nq), (nq, nrope), (nrope, nrope + nkv)],
                      out_t=[True, False, True], name="swa_qkv")
    k = jnp.transpose(k.reshape(t, SWA_KV_HEADS, hd), (1, 0, 2))
    o = _attn("swa", qt, k, vt, dk=hd, dv=hd, ntl=ntl, bias=_swa_mask(ntl), sink=sink.astype(F32) * LOG2E)
    (xa,) = _proj(o, w_o.astype(BF16), ntl=ntl, mods=mods, resid=xa, gate_idx=2,
                  out_dtypes=[F32], name="swa_out")
    return xa


def _mixer_mla(xa, g, mods, w_dq, q_norm, w_uq, w_dkv, kv_norm, w_ukv, w_o, ntl, seq):
    t, d = xa.shape
    nh = MLA_HEADS
    qr = w_dq.shape[1]
    kvr = kv_norm.shape[0]
    nope, rope, dv = MLA_NOPE_DIM, MLA_ROPE_DIM, MLA_V_DIM
    qk = nope + rope
    pad = LANES - qk
    scale = qk ** -0.5 * LOG2E
    kin = kvr + LANES
    w_dn = jnp.concatenate([w_dq, w_dkv, jnp.zeros((d, LANES - rope), F32)], axis=1)
    g_dn = jnp.concatenate([q_norm, kv_norm, jnp.ones((LANES,), F32)])
    cqn, kvin = _proj(xa, w_dn.astype(BF16), ntl=ntl, mods=mods, norm_g=g, mod_idx=(0, 1),
                      rms_segs=[(0, qr), (qr, qr + kvr)], rms_g=g_dn,
                      splits=[(0, qr), (qr, qr + kin)], name="mla_down")
    src, sign = _rope_partner_perm(rope)
    src_head = np.concatenate([np.arange(nope), nope + src, np.arange(qk, LANES)])
    sign_head = np.concatenate([np.zeros(nope, np.float32), sign, np.zeros(pad, np.float32)])
    src_full = (np.arange(nh)[:, None] * LANES + src_head[None, :]).reshape(-1)
    sign_full = np.tile(sign_head, nh)
    cos, sin = _rope_tables(seq, t, rope, [(nope, qk)])
    wq = jnp.pad(w_uq.reshape(qr, nh, qk), ((0, 0), (0, 0), (0, pad))).reshape(qr, nh * LANES)
    wq2 = wq[:, src_full] * sign_full[None, :]
    (qt,) = _proj(cqn, wq.astype(BF16), ntl=ntl, w2=wq2.astype(BF16), cos=cos, sin=sin,
                  colscale=jnp.full((nh * LANES,), scale, F32), out_t=[True], name="mla_q")
    w_ukv3 = w_ukv.reshape(kvr, nh, nope + dv)
    wk_top = jnp.pad(w_ukv3[:, :, :nope], ((0, 0), (0, 0), (0, LANES - nope)))
    eye = jnp.eye(rope, dtype=F32)
    wk_rope = jnp.pad(eye, ((0, LANES - rope), (nope, pad)))
    wk_bot = jnp.broadcast_to(wk_rope[:, None, :], (LANES, nh, LANES))
    wk = jnp.concatenate([wk_top, wk_bot], axis=0).reshape(kin, nh * LANES)
    wk2 = wk[:, src_full] * sign_full[None, :]
    wv = jnp.concatenate([w_ukv3[:, :, nope:].reshape(kvr, nh * dv), jnp.zeros((LANES, nh * dv), F32)], axis=0)
    (k,) = _proj(kvin, wk.astype(BF16), ntl=ntl, w2=wk2.astype(BF16), cos=cos, sin=sin, name="mla_k")
    (vt,) = _proj(kvin, wv.astype(BF16), ntl=ntl, out_t=[True], name="mla_v")
    o = _attn("mla", qt, k, vt, dk=LANES, dv=dv, ntl=ntl)
    (xa,) = _proj(o, w_o.astype(BF16), ntl=ntl, mods=mods, resid=xa, gate_idx=2,
                  out_dtypes=[F32], name="mla_out")
    return xa


def _moe(xa, g, mods, router, w_gate, w_up, w_down, layer, ntl, seq, with_ctx, final_g=None):
    t, d = xa.shape
    ne = router.shape[1]
    nctx = t - seq
    h, aff = _router(xa, g, mods, router.T, ntl)
    cap_l = EC_CAPACITY_FACTOR * seq // ne
    pos_l, gate_l, off_l = _topk(aff[:, :seq].reshape(ne, seq // CHUNK, CHUNK), cap_l, 0)
    per_tile = TB // CHUNK
    pos = [pos_l.reshape(ne, seq)]
    gate = [gate_l.reshape(ne, seq)]
    offs = [off_l[:, ::per_tile, 0]]
    if with_ctx:
        cap_c = EC_CAPACITY_FACTOR * nctx // ne
        cpad = 8 * CHUNK
        aff_c = jnp.concatenate([aff[:, seq:], jnp.full((ne, cpad - nctx), -1.0, F32)], axis=1)
        pos_c, gate_c, off_c = _topk(aff_c.reshape(ne, 8, CHUNK), cap_c, cap_l)
        pos.append(pos_c.reshape(ne, cpad)[:, :nctx])
        gate.append(gate_c.reshape(ne, cpad)[:, :nctx])
        offs.append(off_c[:, 0:nctx // CHUNK:per_tile, 0])
        nslots = cap_l + cap_c
    else:
        pos.append(jnp.full((ne, nctx), -1.0, F32))
        gate.append(jnp.zeros((ne, nctx), F32))
        offs.append(jnp.full((ne, nctx // TB), cap_l, jnp.int32))
        nslots = cap_l
    pos = jnp.concatenate(pos, axis=1)
    gate = jnp.concatenate(gate, axis=1)
    offs = jnp.concatenate(offs + [jnp.full((ne, 1), nslots, jnp.int32)], axis=1)
    cnts = offs[:, 1:] - offs[:, :-1]
    ys = _ffn(offs, pos, gate, h, w_gate, w_up, w_down, layer=layer, nslots=nslots)
    return _combine(offs, cnts, pos.T, xa, mods, ys, ntl, final_g=final_g)


def kernel(x, c, ctx, c_ctx, ada_w, ada_b, norm_mix, norm_ffn, na_w_qkv, na_w_o, na_rpb, swa_w_qkv, swa_w_o, swa_sink, mla_w_dq, mla_q_norm, mla_w_uq, mla_w_dkv, mla_kv_norm, mla_w_ukv, mla_w_o, moe_router, moe_w_gate, moe_w_up, moe_w_down, final_norm):
    assert x.shape[0] == 1 and c.shape[0] == 1 and ctx.shape[0] == 1
    seq, d = x.shape[1], x.shape[2]
    nctx = ctx.shape[1]
    assert seq % TB == 0 and nctx == TB and seq // TB >= 4
    depth = ada_w.shape[0]
    ntl = seq // TB
    xa = jnp.concatenate([x[0], ctx[0]], axis=0)
    cs = jnp.concatenate([c, c_ctx[None, :], jnp.zeros((6, d), F32)], axis=0)
    ada = _ada(cs, ada_w, ada_b)
    mods_all = jnp.transpose(ada[:, :2].reshape(depth, 2, 6, d), (0, 2, 1, 3))[:, :, :, None, :]
    for i in range(depth):
        mods = mods_all[i]
        kind, slot = i % N_MIXERS, i // N_MIXERS
        if kind == 0:
            xa = _mixer_na(xa, norm_mix[i], mods, na_w_qkv[slot], na_w_o[slot], na_rpb[slot], ntl, seq)
        elif kind == 1:
            xa = _mixer_swa(xa, norm_mix[i], mods, swa_w_qkv[slot], swa_w_o[slot], swa_sink[slot], ntl, seq)
        else:
            xa = _mixer_mla(xa, norm_mix[i], mods, mla_w_dq[slot], mla_q_norm[slot], mla_w_uq[slot],
                            mla_w_dkv[slot], mla_kv_norm[slot], mla_w_ukv[slot], mla_w_o[slot], ntl, seq)
        last = i == depth - 1
        xa = _moe(xa, norm_ffn[i], mods, moe_router[i], moe_w_gate, moe_w_up, moe_w_down, i,
                  ntl, seq, with_ctx=not last, final_g=final_norm if last else None)
    return xa[None]
```

```python
import functools
import math

import jax
import jax.numpy as jnp
import numpy as np
from jax import lax
from jax.experimental import pallas as pl
from jax.experimental.pallas import tpu as pltpu

F32 = jnp.float32
BF16 = jnp.bfloat16
HIGHEST = lax.Precision.HIGHEST

GRID_W = 64
N_MIXERS = 3
RMS_EPS = 1e-6
ROPE_THETA = 10000.0
NA_HEADS = 16
NA_WIN_ROWS = 8
NA_WIN_COLS = 16
SWA_Q_HEADS = 16
SWA_KV_HEADS = 4
SWA_WINDOW = 128
MLA_HEADS = 16
MLA_NOPE_DIM = 64
MLA_ROPE_DIM = 32
MLA_V_DIM = 64
N_EXPERTS = 16
EC_CAPACITY_FACTOR = 2

TB = 256
CHUNK = 128
LANES = 128
NEG = -1e30
LOG2E = math.log2(math.e)
FFN_ROWS = 416
FFN_FCHUNK = 512
GATHER_WIN = TB + 16
COMBINE_WIN = 64
VMEM_LIMIT = 56 * 1024 * 1024


def _cparams(*sem):
    return pltpu.CompilerParams(dimension_semantics=sem, vmem_limit_bytes=VMEM_LIMIT)


def _ada_kernel(cs_ref, w_ref, b_ref, o_ref):
    x = cs_ref[...]
    x = x * jax.nn.sigmoid(x)
    y = jnp.dot(x, w_ref[0], precision=HIGHEST, preferred_element_type=F32)
    o_ref[0] = y + b_ref[0]


def _ada(cs, ada_w, ada_b):
    depth, d, n = ada_w.shape
    nb = n // d
    return pl.pallas_call(
        _ada_kernel,
        grid=(depth, nb),
        in_specs=[
            pl.BlockSpec((8, d), lambda i, k: (0, 0)),
            pl.BlockSpec((1, d, d), lambda i, k: (i, 0, k)),
            pl.BlockSpec((1, 1, d), lambda i, k: (i, 0, k)),
        ],
        out_specs=pl.BlockSpec((1, 8, d), lambda i, k: (i, 0, k)),
        out_shape=jax.ShapeDtypeStruct((depth, 8, n), F32),
        compiler_params=_cparams("arbitrary", "arbitrary"),
        name="ada",
    )(cs, ada_w, ada_b.reshape(depth, 1, n))


def _normmod(x, g, shift, scale):
    y = x * lax.rsqrt(jnp.mean(x * x, axis=-1, keepdims=True) + RMS_EPS)
    return (y * g) * (1.0 + scale) + shift


def _proj_kernel(*refs, normmod, rms_segs, rope_cols, colscale, resid, splits, out_t, tail_from):
    it = iter(refs)
    x_ref = next(it)
    if tail_from is not None:
        xt_ref = next(it)
    if normmod:
        g_ref, sh_ref, sc_ref = next(it), next(it), next(it)
    w_ref = next(it)
    if rms_segs:
        g2_ref = next(it)
    if rope_cols:
        w2_ref, cos_ref, sin_ref = next(it), next(it), next(it)
    if colscale:
        cs_ref = next(it)
    if resid:
        res_ref, gate_ref = next(it), next(it)
    out_refs = list(it)

    x = x_ref[...]
    if tail_from is not None:
        x = jnp.where(pl.program_id(0) >= tail_from, xt_ref[...], x)
    if normmod:
        x = _normmod(x.astype(F32), g_ref[...], sh_ref[0, 0], sc_ref[0, 0])
    xb = x.astype(BF16)
    y = jnp.dot(xb, w_ref[...], preferred_element_type=F32)
    if rms_segs:
        parts = []
        prev = 0
        for (a, b) in rms_segs:
            if a > prev:
                parts.append(y[:, prev:a])
            seg = y[:, a:b]
            seg = seg * lax.rsqrt(jnp.mean(seg * seg, axis=-1, keepdims=True) + RMS_EPS)
            parts.append(seg * g2_ref[:, a:b])
            prev = b
        if prev < y.shape[1]:
            parts.append(y[:, prev:])
        y = jnp.concatenate(parts, axis=1)
    if rope_cols:
        y2 = jnp.dot(xb, w2_ref[...], preferred_element_type=F32)
        reps = rope_cols // LANES
        cos = jnp.tile(cos_ref[...], (1, reps))
        sin = jnp.tile(sin_ref[...], (1, reps))
        yr = y[:, :rope_cols] * cos + y2 * sin
        y = yr if rope_cols == y.shape[1] else jnp.concatenate([yr, y[:, rope_cols:]], axis=1)
    if colscale:
        y = y * cs_ref[...]
    if resid:
        y = res_ref[...] + gate_ref[0, 0] * y
    for o_ref, (a, b), tr in zip(out_refs, splits, out_t):
        if tr:
            o_ref[0] = y[:, a:b].T.astype(o_ref.dtype)
        else:
            o_ref[...] = y[:, a:b].astype(o_ref.dtype)


def _proj(x, w, *, ntl, mods=None, norm_g=None, mod_idx=None, rms_segs=None, rms_g=None,
          w2=None, cos=None, sin=None, colscale=None, resid=None, gate_idx=None,
          splits=None, out_dtypes=None, out_t=None, x_tail=None, name="proj"):
    t, kdim = x.shape
    n = w.shape[1]
    nx = t // TB
    nt = nx + (x_tail.shape[0] // TB if x_tail is not None else 0)
    t = nt * TB
    splits = splits or [(0, n)]
    out_dtypes = out_dtypes or [BF16] * len(splits)

    def modspec(k):
        dm = mods.shape[-1]
        return pl.BlockSpec((1, 1, 1, dm), lambda i: (k, i // ntl, 0, 0))

    args, specs = [x], [pl.BlockSpec((TB, kdim), lambda i: (jnp.minimum(i, nx - 1), 0))]
    if x_tail is not None:
        args.append(x_tail)
        specs.append(pl.BlockSpec((TB, kdim), lambda i: (jnp.maximum(i - nx, 0), 0)))
    if norm_g is not None:
        args += [norm_g.reshape(1, kdim), mods, mods]
        specs += [pl.BlockSpec((1, kdim), lambda i: (0, 0)), modspec(mod_idx[0]), modspec(mod_idx[1])]
    args.append(w)
    specs.append(pl.BlockSpec((kdim, n), lambda i: (0, 0)))
    if rms_segs:
        args.append(rms_g.reshape(1, n))
        specs.append(pl.BlockSpec((1, n), lambda i: (0, 0)))
    rope_cols = 0
    if w2 is not None:
        rope_cols = w2.shape[1]
        args += [w2, cos, sin]
        specs += [pl.BlockSpec((kdim, rope_cols), lambda i: (0, 0)),
                  pl.BlockSpec((TB, LANES), lambda i: (i, 0)),
                  pl.BlockSpec((TB, LANES), lambda i: (i, 0))]
    if colscale is not None:
        args.append(colscale.reshape(1, n))
        specs.append(pl.BlockSpec((1, n), lambda i: (0, 0)))
    if resid is not None:
        args += [resid, mods]
        specs += [pl.BlockSpec((TB, n), lambda i: (i, 0)), modspec(gate_idx)]
    out_t = out_t or [False] * len(splits)
    out_shape = [jax.ShapeDtypeStruct((nt, b - a, TB) if tr else (t, b - a), dt)
                 for (a, b), dt, tr in zip(splits, out_dtypes, out_t)]
    out_specs = [pl.BlockSpec((1, b - a, TB), lambda i: (i, 0, 0)) if tr
                 else pl.BlockSpec((TB, b - a), lambda i: (i, 0)) for (a, b), tr in zip(splits, out_t)]
    kern = functools.partial(
        _proj_kernel, normmod=norm_g is not None, rms_segs=rms_segs, rope_cols=rope_cols,
        colscale=colscale is not None, resid=resid is not None, splits=splits, out_t=out_t,
        tail_from=nx if x_tail is not None else None)
    outs = pl.pallas_call(
        kern, grid=(nt,), in_specs=specs, out_specs=out_specs, out_shape=out_shape,
        compiler_params=_cparams("arbitrary"), name=name,
    )(*args)
    return outs


ONES_ROWS = 16


def _softmax_part(st, m):
    m_new = jnp.maximum(m, jnp.max(st, axis=0, keepdims=True))
    return m_new, jnp.exp2(m - m_new), jnp.exp2(st - m_new).astype(BF16)


def _pv_part(vt, p, alpha, acc):
    return alpha * acc + jnp.dot(vt, p, preferred_element_type=F32)


def _softmax_step(st, vt, carry):
    m, acc = carry
    m, alpha, p = _softmax_part(st, m)
    return m, _pv_part(vt, p, alpha, acc)


def _softmax_init(dv):
    return (jnp.full((1, TB), NEG, F32), jnp.zeros((dv + ONES_ROWS, TB), F32))


def _softmax_finish(carry, dv, sink=None):
    m, acc = carry
    num, l = acc[:dv], acc[dv:dv + 1]
    if sink is not None:
        m_f = jnp.maximum(m, sink)
        a = jnp.exp2(m - m_f)
        l = l * a + jnp.exp2(sink - m_f)
        num = num * a
    return (num / l).T


def _scores_t(k, qt):
    return jnp.dot(k, qt, preferred_element_type=F32)


def _attn_kernel(*refs, kind, dk, dv, ntl, hps, nq):
    refs = list(refs)
    sink_ref = refs.pop(0) if kind == "swa" else None
    q_ref, k_ref, vt_ref = refs[:3]
    nbias = nq if kind != "mla" else 0
    bias_refs = refs[3:3 + nbias]
    o_ref = refs[3 + nbias]
    sa_ref, sb_ref, pa_ref, pb_ref = refs[4 + nbias:]
    hp = pl.program_id(0)
    i = pl.program_id(1)
    shared_kv = kind == "swa"
    chains = [(qi, hh) for qi in range(nq) for hh in range(hps)]
    nchain = len(chains)
    qs = [q_ref[qi, hh * dk:(hh + 1) * dk, :] for qi, hh in chains]
    ones = jnp.ones((ONES_ROWS, TB), BF16)

    def scores(c, kt):
        start = pl.multiple_of(kt * TB, TB)
        hh = chains[c][1]
        if shared_kv:
            return _scores_t(k_ref[0, pl.ds(start, TB), :], qs[c])
        return _scores_t(k_ref[pl.ds(start, TB), hh * dk:(hh + 1) * dk], qs[c])

    def vt_tile(c, kt):
        lo = 0 if shared_kv else chains[c][1] * dv
        return jnp.concatenate([vt_ref[kt, lo:lo + dv, :], ones], axis=0)

    states = tuple((m, jnp.ones((1, TB), F32), acc) for m, acc in (_softmax_init(dv) for _ in chains))

    def stage(score_next, kt_prev, s_cur, s_nxt, p_cur, p_prev, states, cs=range(nchain)):
        if score_next is not None:
            for c in cs:
                s_nxt[c] = score_next(c)
        new = list(states)
        for c in cs:
            m, alpha, acc = states[c]
            if p_prev is not None:
                acc = _pv_part(vt_tile(c, kt_prev(c)), p_prev[c], alpha, acc)
            m, alpha, p = _softmax_part(s_cur[c], m)
            p_cur[c] = p
            new[c] = (m, alpha, acc)
        return tuple(new)

    def finish(qi, last_tile, last_p):
        outs = []
        for hh in range(hps):
            c = qi * hps + hh
            m, alpha, acc = states[c]
            acc = _pv_part(vt_tile(c, last_tile(c)), last_p[c], alpha, acc)
            sink = sink_ref[hps * hp + hh] if kind == "swa" else None
            outs.append(_softmax_finish((m, acc), dv, sink))
        o_ref[qi * TB:(qi + 1) * TB, :] = jnp.concatenate(outs, axis=1).astype(o_ref.dtype)

    if kind == "mla":
        def latent(kt):
            return lambda c: scores(c, jnp.minimum(kt, ntl))

        def first_scores(cs):
            for c in cs:
                sa_ref[c] = scores(c, 0)

        groups = [range(qi * hps, (qi + 1) * hps) for qi in range(nq)]
        first_scores(groups[0])
        for qi, cs in enumerate(groups):
            def body(j, states, cs=cs):
                states = stage(latent(2 * j + 2), lambda c: 2 * j, sb_ref, sa_ref, pb_ref, pa_ref, states, cs)
                return stage(latent(2 * j + 3), lambda c: 2 * j + 1, sa_ref, sb_ref, pa_ref, pb_ref, states, cs)

            states = stage(latent(1), None, sa_ref, sb_ref, pa_ref, None, states, cs)
            states = lax.fori_loop(0, ntl // 2, body, states, unroll=16 if ntl % 32 == 0 else 1)
            if qi + 1 < nq:
                first_scores(groups[qi + 1])
            finish(qi, lambda c: ntl, pa_ref)
    else:
        tiles = []
        for qi in range(nq):
            ts = jnp.clip(i * nq + qi - 1, 0, ntl - 3)
            tiles.append([ts, ts + 1, ts + 2, ntl])

        def masked(s):
            def fn(c):
                qi, hh = chains[c]
                st = scores(c, tiles[qi][s])
                if s == 3:
                    return st
                return st + bias_refs[qi][0, hh if kind == "na" else 0, s * TB:(s + 1) * TB, :]
            return fn

        def tile_of(s):
            return lambda c: tiles[chains[c][0]][s]

        for c in range(nchain):
            sa_ref[c] = masked(0)(c)
        states = stage(masked(1), None, sa_ref, sb_ref, pa_ref, None, states)
        states = stage(masked(2), tile_of(0), sb_ref, sa_ref, pb_ref, pa_ref, states)
        states = stage(masked(3), tile_of(1), sa_ref, sb_ref, pa_ref, pb_ref, states)
        states = stage(None, tile_of(2), sb_ref, sa_ref, pb_ref, pa_ref, states)
        for qi in range(nq):
            finish(qi, tile_of(3), pb_ref)


def _attn_ctx_kernel(*refs, dk, dv, hps, has_sink, shared_kv):
    if has_sink:
        sink_ref, q_ref, k_ref, vt_ref, o_ref = refs
    else:
        q_ref, k_ref, vt_ref, o_ref = refs
    hp = pl.program_id(0)
    ones = jnp.ones((ONES_ROWS, TB), BF16)
    outs = []
    for hh in range(hps):
        k = k_ref[0] if shared_kv else k_ref[:, hh * dk:(hh + 1) * dk]
        lo = 0 if shared_kv else hh * dv
        st = _scores_t(k, q_ref[0, hh * dk:(hh + 1) * dk, :])
        vt = jnp.concatenate([vt_ref[0, lo:lo + dv, :], ones], axis=0)
        carry = _softmax_step(st, vt, _softmax_init(dv))
        outs.append(_softmax_finish(carry, dv, sink_ref[hps * hp + hh] if has_sink else None))
    o_ref[...] = jnp.concatenate(outs, axis=1).astype(o_ref.dtype)


def _attn(kind, qt, k, vt, *, dk, dv, ntl, bias=None, sink=None):
    nt, _, _ = qt.shape
    t = nt * TB
    nh = qt.shape[1] // dk
    shared_kv = kind == "swa"
    hps = 2 if kind == "mla" else 4
    kvs = 1 if shared_kv else hps
    nq = 2
    assert not shared_kv or nh // k.shape[0] == hps
    assert ntl % 2 == 0
    kern = functools.partial(_attn_kernel, kind=kind, dk=dk, dv=dv, ntl=ntl, hps=hps, nq=nq)
    local = kind != "mla"
    resident = dict(pipeline_mode=pl.Buffered(1)) if local else {}
    in_specs = [
        pl.BlockSpec((nq, hps * dk, TB), lambda hp, i, *_: (i, hp, 0)),
        pl.BlockSpec((1, t, dk), lambda hp, i, *_: (hp, 0, 0), **resident) if shared_kv
        else pl.BlockSpec((t, hps * dk), lambda hp, i, *_: (0, hp), **resident),
        pl.BlockSpec((nt, kvs * dv, TB), lambda hp, i, *_: (0, hp, 0), **resident),
    ]
    args = [qt, k, vt]
    nsp = 0
    if local:
        per_head = kind == "na"

        def bias_spec(qi):
            def bias_map(hp, i, *_):
                tile = i * nq + qi
                pat = jnp.where(tile == 0, 0, jnp.where(tile >= ntl - 1, 2, 1))
                return (pat, hp if per_head else 0, 0, 0)
            return pl.BlockSpec((1, hps if per_head else 1, 3 * TB, TB), bias_map)
        in_specs += [bias_spec(qi) for qi in range(nq)]
        args += [bias] * nq
    if kind == "swa":
        nsp = 1
        args = [sink] + args
    nchain = nq * hps
    o = pl.pallas_call(
        kern,
        grid_spec=pltpu.PrefetchScalarGridSpec(
            num_scalar_prefetch=nsp, grid=(nh // hps, ntl // nq), in_specs=in_specs,
            out_specs=pl.BlockSpec((nq * TB, hps * dv), lambda hp, i, *_: (i, hp)),
            scratch_shapes=[pltpu.VMEM((nchain, TB, TB), F32)] * 2 + [pltpu.VMEM((nchain, TB, TB), BF16)] * 2),
        out_shape=jax.ShapeDtypeStruct((ntl * TB, nh * dv), BF16),
        compiler_params=_cparams("arbitrary", "arbitrary"), name="attn_" + kind,
    )(*args)
    has_sink = kind == "swa"
    ckern = functools.partial(_attn_ctx_kernel, dk=dk, dv=dv, hps=hps, has_sink=has_sink,
                              shared_kv=shared_kv)
    cargs = ([sink] if has_sink else []) + [qt, k, vt]
    o_ctx = pl.pallas_call(
        ckern,
        grid_spec=pltpu.PrefetchScalarGridSpec(
            num_scalar_prefetch=nsp, grid=(nh // hps,),
            in_specs=[pl.BlockSpec((1, hps * dk, TB), lambda hp, *_: (ntl, hp, 0)),
                      pl.BlockSpec((1, TB, dk), lambda hp, *_: (hp, ntl, 0)) if shared_kv
                      else pl.BlockSpec((TB, hps * dk), lambda hp, *_: (ntl, hp)),
                      pl.BlockSpec((1, kvs * dv, TB), lambda hp, *_: (ntl, hp, 0))],
            out_specs=pl.BlockSpec((TB, hps * dv), lambda hp, *_: (0, hp))),
        out_shape=jax.ShapeDtypeStruct((TB, nh * dv), BF16),
        compiler_params=_cparams("arbitrary"), name="attn_ctx_" + kind,
    )(*cargs)
    return o, o_ctx


def _router_kernel(x_ref, g_ref, sh_ref, sc_ref, rt_ref, h_ref, aff_ref):
    h = _normmod(x_ref[...], g_ref[...], sh_ref[0, 0], sc_ref[0, 0])
    h_ref[...] = h.astype(BF16)
    lg = lax.dot_general(rt_ref[...], h, (((1,), (1,)), ((), ())),
                         precision=HIGHEST, preferred_element_type=F32)
    lg = lg - jnp.max(lg, axis=0, keepdims=True)
    e = jnp.exp(lg)
    aff_ref[...] = e / jnp.sum(e, axis=0, keepdims=True)


def _router(xa, g, mods, router_t, ntl):
    t, d = xa.shape
    ne = router_t.shape[0]

    def modspec(k):
        return pl.BlockSpec((1, 1, 1, d), lambda i: (k, i // ntl, 0, 0))

    return pl.pallas_call(
        _router_kernel, grid=(t // TB,),
        in_specs=[pl.BlockSpec((TB, d), lambda i: (i, 0)),
                  pl.BlockSpec((1, d), lambda i: (0, 0)),
                  modspec(3), modspec(4),
                  pl.BlockSpec((ne, d), lambda i: (0, 0))],
        out_specs=[pl.BlockSpec((TB, d), lambda i: (i, 0)),
                   pl.BlockSpec((ne, TB), lambda i: (0, i))],
        out_shape=[jax.ShapeDtypeStruct((t, d), BF16), jax.ShapeDtypeStruct((ne, t), F32)],
        compiler_params=_cparams("arbitrary"), name="router",
    )(xa, g.reshape(1, d), mods, mods, router_t)


def _topk_kernel(aff_ref, pos_ref, gate_ref, off_ref, *, cap, base):
    x = aff_ref[...]
    ne, nc, _ = x.shape
    bits = lax.bitcast_convert_type(x, jnp.int32)

    def count(mask):
        c = jnp.sum(jnp.where(mask, 1.0, 0.0), axis=1, keepdims=True)
        return jnp.sum(c, axis=2, keepdims=True)

    thr = jnp.zeros((ne, 1, 1), jnp.int32)
    for b in range(30, -1, -1):
        cand = thr | (1 << b)
        thr = jnp.where(count(bits >= cand) >= cap, cand, thr)

    ia = lax.broadcasted_iota(jnp.int32, (LANES, LANES), 0)
    ib = lax.broadcasted_iota(jnp.int32, (LANES, LANES), 1)
    upper = jnp.where(ia <= ib, 1.0, 0.0)
    ones = jnp.ones((LANES, LANES), F32)
    ca = lax.broadcasted_iota(jnp.int32, (nc, nc), 0)
    cb = lax.broadcasted_iota(jnp.int32, (nc, nc), 1)
    lower = jnp.where(cb < ca, 1.0, 0.0)

    def prefix(mask):
        m2 = jnp.where(mask, 1.0, 0.0).reshape(ne * nc, LANES)
        within = jnp.dot(m2, upper, precision=HIGHEST, preferred_element_type=F32)
        tot = jnp.dot(m2, ones, precision=HIGHEST, preferred_element_type=F32).reshape(ne, nc, LANES)
        offs = [jnp.dot(lower, tot[e], precision=HIGHEST, preferred_element_type=F32)[None]
                for e in range(ne)]
        off = jnp.concatenate(offs, axis=0)
        return within.reshape(ne, nc, LANES) + off, off

    gt = bits > thr
    eq = bits == thr
    need = cap - count(gt)
    eq_rank, _ = prefix(eq)
    sel = gt | (eq & (eq_rank <= need))
    sel_rank, off = prefix(sel)
    pos_ref[...] = jnp.where(sel, sel_rank - 1.0 + base, -1.0)
    gate_ref[...] = jnp.where(sel, x, 0.0)
    off_ref[...] = off.astype(jnp.int32) + base


def _topk(aff3, cap, base):
    ne, nc, _ = aff3.shape
    kern = functools.partial(_topk_kernel, cap=cap, base=base)
    spec = pl.BlockSpec((ne, nc, LANES), lambda i: (0, 0, 0))
    return pl.pallas_call(
        kern, grid=(1,), in_specs=[spec], out_specs=[spec, spec, spec],
        out_shape=[jax.ShapeDtypeStruct(aff3.shape, F32), jax.ShapeDtypeStruct(aff3.shape, F32),
                   jax.ShapeDtypeStruct(aff3.shape, jnp.int32)],
        compiler_params=_cparams("arbitrary"), name="topk",
    )(aff3)


def _ffn_kernel(offs_ref, pos_ref, gate_ref, h_ref, wg_ref, wu_ref, wd_ref, y_ref,
                xs_ref, xb_ref, gs_ref, *, ng, gt, nf, nrows):
    e = pl.program_id(0)
    j = pl.program_id(1)

    @pl.when(j == 0)
    def _():
        xs_ref[...] = jnp.zeros_like(xs_ref)
        gs_ref[...] = jnp.zeros_like(gs_ref)

    @pl.when(j < ng)
    def _():
        for s in range(gt):
            off = offs_ref[e, j * gt + s]
            w0 = pl.multiple_of((off // 8) * 8, 8)
            prow = pos_ref[0, :, s * TB:(s + 1) * TB]
            grow = gate_ref[0, :, s * TB:(s + 1) * TB]
            ids = (w0 + lax.broadcasted_iota(jnp.int32, (GATHER_WIN, 1), 0)).astype(F32)
            hit = prow == ids
            onehot = jnp.where(hit, 1.0, 0.0).astype(BF16)
            xs_ref[pl.ds(w0, GATHER_WIN), :] += jnp.dot(
                onehot, h_ref[s * TB:(s + 1) * TB, :], preferred_element_type=F32)
            gsel = jnp.sum(jnp.where(hit, grow, 0.0), axis=1, keepdims=True)
            gs_ref[pl.ds(w0, GATHER_WIN), :] += jnp.broadcast_to(gsel, (GATHER_WIN, LANES))

    @pl.when(j == ng)
    def _():
        xb_ref[...] = xs_ref[0:nrows, :].astype(BF16)
        xs_ref[...] = jnp.zeros_like(xs_ref)

    @pl.when(j >= ng)
    def _():
        wg = wg_ref[0, 0].astype(BF16)
        wu = wu_ref[0, 0].astype(BF16)
        wd = wd_ref[0, 0].astype(BF16)
        for b in range(nrows // FFN_ROWS):
            rows = slice(b * FFN_ROWS, (b + 1) * FFN_ROWS)
            x = xb_ref[rows, :]
            a = jnp.dot(x, wg, preferred_element_type=F32)
            u = jnp.dot(x, wu, preferred_element_type=F32)
            hmid = (a * jax.nn.sigmoid(a) * u).astype(BF16)
            xs_ref[rows, :] += jnp.dot(hmid, wd, preferred_element_type=F32)

    @pl.when(j == ng + nf - 1)
    def _():
        y_ref[0, 0:nrows, :] = (xs_ref[0:nrows, :] * gs_ref[0:nrows, 0:1]).astype(y_ref.dtype)
        if y_ref.shape[1] > nrows:
            y_ref[0, nrows:, :] = jnp.zeros((y_ref.shape[1] - nrows, y_ref.shape[2]), y_ref.dtype)


def _ffn(offs, pos, gate, h, wg, wu, wd, *, layer, nslots):
    t, d = h.shape
    nt = t // TB
    _, ne, _, f = wg.shape
    gt = max(g for g in range(1, 17) if nt % g == 0)
    ng = nt // gt
    nf = f // FFN_FCHUNK
    nrows = -(-nslots // FFN_ROWS) * FFN_ROWS
    srows = -(-(nslots + TB) // 16) * 16
    srows = max(srows, nrows)
    xs_rows = max(nrows, nslots + GATHER_WIN)
    kern = functools.partial(_ffn_kernel, ng=ng, gt=gt, nf=nf, nrows=nrows)

    def fchunk(j):
        return jnp.clip(j - ng, 0, nf - 1)

    grid_spec = pltpu.PrefetchScalarGridSpec(
        num_scalar_prefetch=1, grid=(ne, ng + nf),
        in_specs=[
            pl.BlockSpec((1, 1, gt * TB), lambda e, j, o: (e, 0, jnp.minimum(j, ng - 1))),
            pl.BlockSpec((1, 1, gt * TB), lambda e, j, o: (e, 0, jnp.minimum(j, ng - 1))),
            pl.BlockSpec((gt * TB, d), lambda e, j, o: (jnp.minimum(j, ng - 1), 0)),
            pl.BlockSpec((1, 1, d, FFN_FCHUNK), lambda e, j, o: (layer, e, 0, fchunk(j))),
            pl.BlockSpec((1, 1, d, FFN_FCHUNK), lambda e, j, o: (layer, e, 0, fchunk(j))),
            pl.BlockSpec((1, 1, FFN_FCHUNK, d), lambda e, j, o: (layer, e, fchunk(j), 0)),
        ],
        out_specs=pl.BlockSpec((1, srows, d), lambda e, j, o: (e, 0, 0)),
        scratch_shapes=[pltpu.VMEM((xs_rows, d), F32), pltpu.VMEM((nrows, d), BF16),
                        pltpu.VMEM((xs_rows, LANES), F32)])
    return pl.pallas_call(
        kern, grid_spec=grid_spec,
        out_shape=jax.ShapeDtypeStruct((ne, srows, d), BF16),
        compiler_params=_cparams("arbitrary", "arbitrary"), name="ffn",
    )(offs, pos.reshape(ne, 1, t), gate.reshape(ne, 1, t), h, wg, wu, wd)


def _combine_kernel(*refs, ne, srows, final):
    offs_ref, cnt_ref, posc_ref, xa_ref, gate_ref, ys_ref = refs[:6]
    fg_ref = refs[6] if final else None
    o_ref, win_buf, spill_buf, win_sem, spill_sem = refs[6 + (1 if final else 0):]
    j = pl.program_id(0)
    slot = j % 2

    def window_copy(e, tile, buf_slot):
        row0 = pl.multiple_of(e * srows + (offs_ref[e, tile] // 16) * 16, 16)
        return pltpu.make_async_copy(
            ys_ref.at[pl.ds(row0, COMBINE_WIN), :],
            win_buf.at[buf_slot, pl.ds(e * COMBINE_WIN, COMBINE_WIN), :], win_sem.at[buf_slot])

    @pl.when(j == 0)
    def _():
        for e in range(ne):
            window_copy(e, 0, 0).start()

    @pl.when(j + 1 < pl.num_programs(0))
    def _():
        for e in range(ne):
            window_copy(e, j + 1, 1 - slot).start()

    for e in range(ne):
        window_copy(e, j, slot).wait()

    lane = lax.broadcasted_iota(jnp.int32, (1, LANES), 1)
    first = lane < COMBINE_WIN
    blocks = []
    for e in range(0, ne, 2):
        w0a = (offs_ref[e, j] // 16) * 16
        w0b = (offs_ref[e + 1, j] // 16) * 16
        ids = jnp.where(first, w0a + lane, w0b + lane - COMBINE_WIN).astype(F32)
        pcol = jnp.where(first, posc_ref[:, e:e + 1], posc_ref[:, e + 1:e + 2])
        blocks.append(jnp.where(pcol == ids, 1.0, 0.0).astype(BF16))
    onehot = jnp.concatenate(blocks, axis=1)
    acc = jnp.dot(onehot, win_buf[slot], preferred_element_type=F32)
    o_ref[...] = xa_ref[...] + gate_ref[0, 0] * acc

    wide = lax.broadcasted_iota(jnp.int32, (1, TB), 1)
    for e in range(ne):
        off = offs_ref[e, j]
        w0 = (off // 16) * 16

        @pl.when(off - w0 + cnt_ref[e, j] > COMBINE_WIN)
        def _(e=e, w0=w0):
            w1 = jnp.minimum(w0 + COMBINE_WIN, srows - TB)
            spill = pltpu.make_async_copy(
                ys_ref.at[pl.ds(pl.multiple_of(e * srows + w1, 16), TB), :], spill_buf, spill_sem)
            spill.start()
            spill.wait()
            ids = w1 + wide
            hit = (posc_ref[:, e:e + 1] == ids.astype(F32)) & (ids >= w0 + COMBINE_WIN)
            oh = jnp.where(hit, 1.0, 0.0).astype(BF16)
            o_ref[...] += gate_ref[0, 0] * jnp.dot(oh, spill_buf[...], preferred_element_type=F32)

    if final:
        x = o_ref[...]
        o_ref[...] = x * lax.rsqrt(jnp.mean(x * x, axis=-1, keepdims=True) + RMS_EPS) * fg_ref[...]


def _combine(offs, cnts, pos_cols, xa, mods, ys, ntl, final_g=None):
    t, d = xa.shape
    nt = ntl if final_g is not None else t // TB
    ne, srows, _ = ys.shape
    assert srows % 16 == 0
    ys2 = ys.reshape(ne * srows, d)

    assert ne % 2 == 0 and 2 * COMBINE_WIN == LANES
    in_specs = [
        pl.BlockSpec((TB, ne), lambda j, o, c: (j, 0)),
        pl.BlockSpec((TB, d), lambda j, o, c: (j, 0)),
        pl.BlockSpec((1, 1, 1, d), lambda j, o, c: (5, j // ntl, 0, 0)),
        pl.BlockSpec(memory_space=pl.ANY),
    ]
    args = [offs, cnts, pos_cols, xa, mods, ys2]
    if final_g is not None:
        in_specs.append(pl.BlockSpec((1, d), lambda j, o, c: (0, 0)))
        args.append(final_g.reshape(1, d))
    grid_spec = pltpu.PrefetchScalarGridSpec(
        num_scalar_prefetch=2, grid=(nt,), in_specs=in_specs,
        out_specs=pl.BlockSpec((TB, d), lambda j, o, c: (j, 0)),
        scratch_shapes=[pltpu.VMEM((2, ne * COMBINE_WIN, d), BF16), pltpu.VMEM((TB, d), BF16),
                        pltpu.SemaphoreType.DMA((2,)), pltpu.SemaphoreType.DMA(())])
    kern = functools.partial(_combine_kernel, ne=ne, srows=srows, final=final_g is not None)
    return pl.pallas_call(
        kern, grid_spec=grid_spec, out_shape=jax.ShapeDtypeStruct((nt * TB, d), F32),
        compiler_params=_cparams("arbitrary"), name="combine",
    )(*args)


def _rope_tables(seq, t, rot_dim, lane_layout):
    quarter = rot_dim // 4
    rows = seq // GRID_W
    inv_freq = ROPE_THETA ** (-jnp.arange(quarter, dtype=F32) / quarter)
    ang_r = jnp.arange(rows, dtype=F32)[:, None] * inv_freq
    ang_c = jnp.arange(GRID_W, dtype=F32)[:, None] * inv_freq

    def table(fn, fill):
        r = jnp.broadcast_to(fn(ang_r)[:, None, :], (rows, GRID_W, quarter))
        c = jnp.broadcast_to(fn(ang_c)[None, :, :], (rows, GRID_W, quarter))
        blk = jnp.concatenate([r, r, c, c], axis=2)
        parts, prev = [], 0
        for lo, hi in lane_layout:
            if lo > prev:
                parts.append(jnp.full((rows, GRID_W, lo - prev), fill, F32))
            parts.append(blk)
            prev = hi
        if prev < LANES:
            parts.append(jnp.full((rows, GRID_W, LANES - prev), fill, F32))
        lat = jnp.concatenate(parts, axis=2).reshape(seq, LANES)
        return jnp.concatenate([lat, jnp.full((t - seq, LANES), fill, F32)], axis=0)

    return table(jnp.cos, 1.0), table(jnp.sin, 0.0)


def _rope_partner_perm(rot_dim):
    q = rot_dim // 4
    src = np.concatenate([np.arange(q, 2 * q), np.arange(0, q), np.arange(3 * q, 4 * q), np.arange(2 * q, 3 * q)])
    sign = np.concatenate([-np.ones(q), np.ones(q), -np.ones(q), np.ones(q)]).astype(np.float32)
    return src, sign


def _swa_mask(ntl):
    out = []
    for i in (0, 1, ntl - 1):
        ts = min(max(i - 1, 0), ntl - 3)
        kpos = ts * TB + np.arange(3 * TB)[:, None]
        qpos = i * TB + np.arange(TB)[None, :]
        out.append(np.where(np.abs(kpos - qpos) <= SWA_WINDOW, 0.0, NEG).astype(np.float32))
    return jnp.asarray(np.stack(out)[:, None])


def _na_bias(rpb, rows, ntl):
    nh = rpb.shape[0]
    rpt = TB // GRID_W
    kr = min(NA_WIN_ROWS, rows)
    c = np.arange(GRID_W)
    cs = np.clip(c - NA_WIN_COLS // 2, 0, GRID_W - NA_WIN_COLS)
    okc = (c[:, None] >= cs[None, :]) & (c[:, None] < cs[None, :] + NA_WIN_COLS)
    dc = np.clip(c[:, None] - c[None, :] + NA_WIN_COLS - 1, 0, 2 * NA_WIN_COLS - 2)
    toep = jnp.where(okc[None, None], rpb[:, :, dc] * LOG2E, NEG)
    neg = jnp.full((nh, GRID_W, GRID_W), NEG, F32)
    out = []
    for i in (0, 1, ntl - 1):
        ts = min(max(i - 1, 0), ntl - 3)
        key_rows = []
        for krow in range(3 * rpt):
            r2 = ts * rpt + krow
            blocks = []
            for qrow in range(rpt):
                r = i * rpt + qrow
                rs = min(max(r - kr // 2, 0), rows - kr)
                blocks.append(toep[:, r2 - r + NA_WIN_ROWS - 1] if rs <= r2 < rs + kr else neg)
            key_rows.append(jnp.concatenate(blocks, axis=2))
        out.append(jnp.concatenate(key_rows, axis=1))
    return jnp.stack(out, axis=0).astype(F32)


def _mixer_na(xa, g, mods, w_qkv, w_o, rpb, ntl, seq):
    d = xa.shape[1]
    hd = d // NA_HEADS
    n = NA_HEADS * hd
    scale = hd ** -0.5 * LOG2E
    colscale = jnp.concatenate([jnp.full((n,), scale, F32), jnp.ones((2 * n,), F32)])
    qt, k, vt = _proj(xa, w_qkv.astype(BF16), ntl=ntl, mods=mods, norm_g=g, mod_idx=(0, 1),
                      colscale=colscale, splits=[(0, n), (n, 2 * n), (2 * n, 3 * n)],
                      out_t=[True, False, True], name="na_qkv")
    bias = _na_bias(rpb, seq // GRID_W, ntl)
    o, o_ctx = _attn("na", qt, k, vt, dk=hd, dv=hd, ntl=ntl, bias=bias)
    (xa,) = _proj(o, w_o.astype(BF16), x_tail=o_ctx, ntl=ntl, mods=mods, resid=xa, gate_idx=2,
                  out_dtypes=[F32], name="na_out")
    return xa


def _mixer_swa(xa, g, mods, w_qkv, w_o, sink, ntl, seq):
    t, d = xa.shape
    hd = d // SWA_Q_HEADS
    nq, nkv = SWA_Q_HEADS * hd, SWA_KV_HEADS * hd
    scale = hd ** -0.5 * LOG2E
    nrope = nq + nkv
    src, sign = _rope_partner_perm(hd)
    nblk = nrope // hd
    src_full = (np.arange(nblk)[:, None] * hd + src[None, :]).reshape(-1)
    sign_full = np.tile(sign, nblk)
    w2 = w_qkv[:, src_full] * sign_full[None, :]
    cos, sin = _rope_tables(seq, t, hd, [(0, hd), (hd, 2 * hd)])
    colscale = jnp.concatenate([jnp.full((nq,), scale, F32), jnp.ones((2 * nkv,), F32)])
    qt, k, vt = _proj(xa, w_qkv.astype(BF16), ntl=ntl, mods=mods, norm_g=g, mod_idx=(0, 1),
                      w2=w2.astype(BF16), cos=cos, sin=sin, colscale=colscale,
                      splits=[(0, nq), (nq, nrope), (nrope, nrope + nkv)],
                      out_t=[True, False, True], name="swa_qkv")
    k = jnp.transpose(k.reshape(t, SWA_KV_HEADS, hd), (1, 0, 2))
    o, o_ctx = _attn("swa", qt, k, vt, dk=hd, dv=hd, ntl=ntl, bias=_swa_mask(ntl),
                     sink=sink.astype(F32) * LOG2E)
    (xa,) = _proj(o, w_o.astype(BF16), x_tail=o_ctx, ntl=ntl, mods=mods, resid=xa, gate_idx=2,
                  out_dtypes=[F32], name="swa_out")
    return xa


def _mixer_mla(xa, g, mods, w_dq, q_norm, w_uq, w_dkv, kv_norm, w_ukv, w_o, ntl, seq):
    t, d = xa.shape
    nh = MLA_HEADS
    qr = w_dq.shape[1]
    kvr = kv_norm.shape[0]
    nope, rope, dv = MLA_NOPE_DIM, MLA_ROPE_DIM, MLA_V_DIM
    qk = nope + rope
    pad = LANES - qk
    scale = qk ** -0.5 * LOG2E
    kin = kvr + LANES
    w_dn = jnp.concatenate([w_dq, w_dkv, jnp.zeros((d, LANES - rope), F32)], axis=1)
    g_dn = jnp.concatenate([q_norm, kv_norm, jnp.ones((LANES,), F32)])
    cqn, kvin = _proj(xa, w_dn.astype(BF16), ntl=ntl, mods=mods, norm_g=g, mod_idx=(0, 1),
                      rms_segs=[(0, qr), (qr, qr + kvr)], rms_g=g_dn,
                      splits=[(0, qr), (qr, qr + kin)], name="mla_down")
    src, sign = _rope_partner_perm(rope)
    src_head = np.concatenate([np.arange(nope), nope + src, np.arange(qk, LANES)])
    sign_head = np.concatenate([np.zeros(nope, np.float32), sign, np.zeros(pad, np.float32)])
    src_full = (np.arange(nh)[:, None] * LANES + src_head[None, :]).reshape(-1)
    sign_full = np.tile(sign_head, nh)
    cos, sin = _rope_tables(seq, t, rope, [(nope, qk)])
    wq = jnp.pad(w_uq.reshape(qr, nh, qk), ((0, 0), (0, 0), (0, pad))).reshape(qr, nh * LANES)
    wq2 = wq[:, src_full] * sign_full[None, :]
    (qt,) = _proj(cqn, wq.astype(BF16), ntl=ntl, w2=wq2.astype(BF16), cos=cos, sin=sin,
                  colscale=jnp.full((nh * LANES,), scale, F32), out_t=[True], name="mla_q")
    w_ukv3 = w_ukv.reshape(kvr, nh, nope + dv)
    wk_top = jnp.pad(w_ukv3[:, :, :nope], ((0, 0), (0, 0), (0, LANES - nope)))
    eye = jnp.eye(rope, dtype=F32)
    wk_rope = jnp.pad(eye, ((0, LANES - rope), (nope, pad)))
    wk_bot = jnp.broadcast_to(wk_rope[:, None, :], (LANES, nh, LANES))
    wk = jnp.concatenate([wk_top, wk_bot], axis=0).reshape(kin, nh * LANES)
    wk2 = wk[:, src_full] * sign_full[None, :]
    wv = jnp.concatenate([w_ukv3[:, :, nope:].reshape(kvr, nh * dv), jnp.zeros((LANES, nh * dv), F32)], axis=0)
    (k,) = _proj(kvin, wk.astype(BF16), ntl=ntl, w2=wk2.astype(BF16), cos=cos, sin=sin, name="mla_k")
    (vt,) = _proj(kvin, wv.astype(BF16), ntl=ntl, out_t=[True], name="mla_v")
    o, o_ctx = _attn("mla", qt, k, vt, dk=LANES, dv=dv, ntl=ntl)
    (xa,) = _proj(o, w_o.astype(BF16), x_tail=o_ctx, ntl=ntl, mods=mods, resid=xa, gate_idx=2,
                  out_dtypes=[F32], name="mla_out")
    return xa


def _moe(xa, g, mods, router, w_gate, w_up, w_down, layer, ntl, seq, with_ctx, final_g=None):
    t, d = xa.shape
    ne = router.shape[1]
    nctx = t - seq
    h, aff = _router(xa, g, mods, router.T, ntl)
    cap_l = EC_CAPACITY_FACTOR * seq // ne
    pos_l, gate_l, off_l = _topk(aff[:, :seq].reshape(ne, seq // CHUNK, CHUNK), cap_l, 0)
    per_tile = TB // CHUNK
    pos = [pos_l.reshape(ne, seq)]
    gate = [gate_l.reshape(ne, seq)]
    offs = [off_l[:, ::per_tile, 0]]
    if with_ctx:
        cap_c = EC_CAPACITY_FACTOR * nctx // ne
        cpad = 8 * CHUNK
        aff_c = jnp.concatenate([aff[:, seq:], jnp.full((ne, cpad - nctx), -1.0, F32)], axis=1)
        pos_c, gate_c, off_c = _topk(aff_c.reshape(ne, 8, CHUNK), cap_c, cap_l)
        pos.append(pos_c.reshape(ne, cpad)[:, :nctx])
        gate.append(gate_c.reshape(ne, cpad)[:, :nctx])
        offs.append(off_c[:, 0:nctx // CHUNK:per_tile, 0])
        nslots = cap_l + cap_c
    else:
        pos.append(jnp.full((ne, nctx), -1.0, F32))
        gate.append(jnp.zeros((ne, nctx), F32))
        offs.append(jnp.full((ne, nctx // TB), cap_l, jnp.int32))
        nslots = cap_l
    pos = jnp.concatenate(pos, axis=1)
    gate = jnp.concatenate(gate, axis=1)
    offs = jnp.concatenate(offs + [jnp.full((ne, 1), nslots, jnp.int32)], axis=1)
    cnts = offs[:, 1:] - offs[:, :-1]
    ys = _ffn(offs, pos, gate, h, w_gate, w_up, w_down, layer=layer, nslots=nslots)
    return _combine(offs, cnts, pos.T, xa, mods, ys, ntl, final_g=final_g)


def kernel(x, c, ctx, c_ctx, ada_w, ada_b, norm_mix, norm_ffn, na_w_qkv, na_w_o, na_rpb, swa_w_qkv, swa_w_o, swa_sink, mla_w_dq, mla_q_norm, mla_w_uq, mla_w_dkv, mla_kv_norm, mla_w_ukv, mla_w_o, moe_router, moe_w_gate, moe_w_up, moe_w_down, final_norm):
    assert x.shape[0] == 1 and c.shape[0] == 1 and ctx.shape[0] == 1
    seq, d = x.shape[1], x.shape[2]
    nctx = ctx.shape[1]
    assert seq % TB == 0 and nctx == TB and seq // TB >= 4
    depth = ada_w.shape[0]
    ntl = seq // TB
    xa = jnp.concatenate([x[0], ctx[0]], axis=0)
    cs = jnp.concatenate([c, c_ctx[None, :], jnp.zeros((6, d), F32)], axis=0)
    ada = _ada(cs, ada_w, ada_b)
    mods_all = jnp.transpose(ada[:, :2].reshape(depth, 2, 6, d), (0, 2, 1, 3))[:, :, :, None, :]
    for i in range(depth):
        mods = mods_all[i]
        kind, slot = i % N_MIXERS, i // N_MIXERS
        if kind == 0:
            xa = _mixer_na(xa, norm_mix[i], mods, na_w_qkv[slot], na_w_o[slot], na_rpb[slot], ntl, seq)
        elif kind == 1:
            xa = _mixer_swa(xa, norm_mix[i], mods, swa_w_qkv[slot], swa_w_o[slot], swa_sink[slot], ntl, seq)
        else:
            xa = _mixer_mla(xa, norm_mix[i], mods, mla_w_dq[slot], mla_q_norm[slot], mla_w_uq[slot],
                            mla_w_dkv[slot], mla_kv_norm[slot], mla_w_ukv[slot], mla_w_o[slot], ntl, seq)
        last = i == depth - 1
        xa = _moe(xa, norm_ffn[i], mods, moe_router[i], moe_w_gate, moe_w_up, moe_w_down, i,
                  ntl, seq, with_ctx=not last, final_g=final_norm if last else None)
    return xa[None]
```

```python
import functools
import math

import jax
import jax.numpy as jnp
import numpy as np
from jax import lax
from jax.experimental import pallas as pl
from jax.experimental.pallas import tpu as pltpu

F32 = jnp.float32
BF16 = jnp.bfloat16
HIGHEST = lax.Precision.HIGHEST

GRID_W = 64
N_MIXERS = 3
RMS_EPS = 1e-6
ROPE_THETA = 10000.0
NA_HEADS = 16
NA_WIN_ROWS = 8
NA_WIN_COLS = 16
SWA_Q_HEADS = 16
SWA_KV_HEADS = 4
SWA_WINDOW = 128
MLA_HEADS = 16
MLA_NOPE_DIM = 64
MLA_ROPE_DIM = 32
MLA_V_DIM = 64
N_EXPERTS = 16
EC_CAPACITY_FACTOR = 2

TB = 256
CHUNK = 128
LANES = 128
ROWS_F32 = 8
ROWS_BF16 = 16
NEG = -1e30
LOG2E = math.log2(math.e)
FFN_ROWS = 416
FFN_FCHUNK = 512
GATHER_WIN = TB + 2 * ROWS_F32
COMBINE_WIN = 64
VMEM_LIMIT = 56 * 1024 * 1024


def _cparams(*sem):
    return pltpu.CompilerParams(dimension_semantics=sem, vmem_limit_bytes=VMEM_LIMIT)


def _ada_kernel(cs_ref, w_ref, b_ref, o_ref):
    x = cs_ref[...]
    x = x * jax.nn.sigmoid(x)
    y = jnp.dot(x, w_ref[0], precision=HIGHEST, preferred_element_type=F32)
    o_ref[0] = y + b_ref[0]


def _ada(cs, ada_w, ada_b):
    depth, d, n = ada_w.shape
    nb = n // d
    return pl.pallas_call(
        _ada_kernel,
        grid=(depth, nb),
        in_specs=[
            pl.BlockSpec((8, d), lambda i, k: (0, 0)),
            pl.BlockSpec((1, d, d), lambda i, k: (i, 0, k)),
            pl.BlockSpec((1, 1, d), lambda i, k: (i, 0, k)),
        ],
        out_specs=pl.BlockSpec((1, 8, d), lambda i, k: (i, 0, k)),
        out_shape=jax.ShapeDtypeStruct((depth, 8, n), F32),
        compiler_params=_cparams("arbitrary", "arbitrary"),
        name="ada",
    )(cs, ada_w, ada_b.reshape(depth, 1, n))


def _normmod(x, g, shift, scale):
    y = x * lax.rsqrt(jnp.mean(x * x, axis=-1, keepdims=True) + RMS_EPS)
    return (y * g) * (1.0 + scale) + shift


def _proj_kernel(*refs, normmod, rms_segs, rope_cols, colscale, resid, splits, out_t, tail_from):
    it = iter(refs)
    x_ref = next(it)
    if tail_from is not None:
        xt_ref = next(it)
    if normmod:
        g_ref, sh_ref, sc_ref = next(it), next(it), next(it)
    w_ref = next(it)
    if rms_segs:
        g2_ref = next(it)
    if rope_cols:
        w2_ref, cos_ref, sin_ref = next(it), next(it), next(it)
    if colscale:
        cs_ref = next(it)
    if resid:
        res_ref, gate_ref = next(it), next(it)
    out_refs = list(it)

    x = x_ref[...]
    if tail_from is not None:
        x = jnp.where(pl.program_id(0) >= tail_from, xt_ref[...], x)
    if normmod:
        x = _normmod(x.astype(F32), g_ref[...], sh_ref[0, 0], sc_ref[0, 0])
    xb = x.astype(BF16)
    y = jnp.dot(xb, w_ref[...], preferred_element_type=F32)
    if rms_segs:
        parts = []
        prev = 0
        for (a, b) in rms_segs:
            if a > prev:
                parts.append(y[:, prev:a])
            seg = y[:, a:b]
            seg = seg * lax.rsqrt(jnp.mean(seg * seg, axis=-1, keepdims=True) + RMS_EPS)
            parts.append(seg * g2_ref[:, a:b])
            prev = b
        if prev < y.shape[1]:
            parts.append(y[:, prev:])
        y = jnp.concatenate(parts, axis=1)
    if rope_cols:
        y2 = jnp.dot(xb, w2_ref[...], preferred_element_type=F32)
        reps = rope_cols // LANES
        cos = jnp.tile(cos_ref[...], (1, reps))
        sin = jnp.tile(sin_ref[...], (1, reps))
        yr = y[:, :rope_cols] * cos + y2 * sin
        y = yr if rope_cols == y.shape[1] else jnp.concatenate([yr, y[:, rope_cols:]], axis=1)
    if colscale:
        y = y * cs_ref[...]
    if resid:
        y = res_ref[...] + gate_ref[0, 0] * y
    for o_ref, (a, b), tr in zip(out_refs, splits, out_t):
        if tr:
            o_ref[0] = y[:, a:b].T.astype(o_ref.dtype)
        else:
            o_ref[...] = y[:, a:b].astype(o_ref.dtype)


def _proj(x, w, *, ntl, mods=None, norm_g=None, mod_idx=None, rms_segs=None, rms_g=None,
          w2=None, cos=None, sin=None, colscale=None, resid=None, gate_idx=None,
          splits=None, out_dtypes=None, out_t=None, x_tail=None, name="proj"):
    t, kdim = x.shape
    n = w.shape[1]
    nx = t // TB
    nt = nx + (x_tail.shape[0] // TB if x_tail is not None else 0)
    t = nt * TB
    splits = splits or [(0, n)]
    out_dtypes = out_dtypes or [BF16] * len(splits)

    def modspec(k):
        dm = mods.shape[-1]
        return pl.BlockSpec((1, 1, 1, dm), lambda i: (k, i // ntl, 0, 0))

    args, specs = [x], [pl.BlockSpec((TB, kdim), lambda i: (jnp.minimum(i, nx - 1), 0))]
    if x_tail is not None:
        args.append(x_tail)
        specs.append(pl.BlockSpec((TB, kdim), lambda i: (jnp.maximum(i - nx, 0), 0)))
    if norm_g is not None:
        args += [norm_g.reshape(1, kdim), mods, mods]
        specs += [pl.BlockSpec((1, kdim), lambda i: (0, 0)), modspec(mod_idx[0]), modspec(mod_idx[1])]
    args.append(w)
    specs.append(pl.BlockSpec((kdim, n), lambda i: (0, 0)))
    if rms_segs:
        args.append(rms_g.reshape(1, n))
        specs.append(pl.BlockSpec((1, n), lambda i: (0, 0)))
    rope_cols = 0
    if w2 is not None:
        rope_cols = w2.shape[1]
        args += [w2, cos, sin]
        specs += [pl.BlockSpec((kdim, rope_cols), lambda i: (0, 0)),
                  pl.BlockSpec((TB, LANES), lambda i: (i, 0)),
                  pl.BlockSpec((TB, LANES), lambda i: (i, 0))]
    if colscale is not None:
        args.append(colscale.reshape(1, n))
        specs.append(pl.BlockSpec((1, n), lambda i: (0, 0)))
    if resid is not None:
        args += [resid, mods]
        specs += [pl.BlockSpec((TB, n), lambda i: (i, 0)), modspec(gate_idx)]
    out_t = out_t or [False] * len(splits)
    out_shape = [jax.ShapeDtypeStruct((nt, b - a, TB) if tr else (t, b - a), dt)
                 for (a, b), dt, tr in zip(splits, out_dtypes, out_t)]
    out_specs = [pl.BlockSpec((1, b - a, TB), lambda i: (i, 0, 0)) if tr
                 else pl.BlockSpec((TB, b - a), lambda i: (i, 0)) for (a, b), tr in zip(splits, out_t)]
    kern = functools.partial(
        _proj_kernel, normmod=norm_g is not None, rms_segs=rms_segs, rope_cols=rope_cols,
        colscale=colscale is not None, resid=resid is not None, splits=splits, out_t=out_t,
        tail_from=nx if x_tail is not None else None)
    outs = pl.pallas_call(
        kern, grid=(nt,), in_specs=specs, out_specs=out_specs, out_shape=out_shape,
        compiler_params=_cparams("arbitrary"), name=name,
    )(*args)
    return outs


ONES_ROWS = 16


def _softmax_part(st, m):
    m_new = jnp.maximum(m, jnp.max(st, axis=0, keepdims=True))
    return m_new, jnp.exp2(m - m_new), jnp.exp2(st - m_new).astype(BF16)


def _pv_part(vt, p, alpha, acc):
    return alpha * acc + jnp.dot(vt, p, preferred_element_type=F32)


def _softmax_step(st, vt, carry):
    m, acc = carry
    m, alpha, p = _softmax_part(st, m)
    return m, _pv_part(vt, p, alpha, acc)


def _softmax_init(dv):
    return (jnp.full((1, TB), NEG, F32), jnp.zeros((dv + ONES_ROWS, TB), F32))


def _softmax_finish(carry, dv, sink=None):
    m, acc = carry
    num, l = acc[:dv], acc[dv:dv + 1]
    if sink is not None:
        m_f = jnp.maximum(m, sink)
        a = jnp.exp2(m - m_f)
        l = l * a + jnp.exp2(sink - m_f)
        num = num * a
    return (num / l).T


def _scores_t(k, qt):
    return jnp.dot(k, qt, preferred_element_type=F32)


def _attn_kernel(*refs, kind, dk, dv, ntl, hps, nq):
    refs = list(refs)
    sink_ref = refs.pop(0) if kind == "swa" else None
    q_ref, k_ref, vt_ref = refs[:3]
    nbias = nq if kind != "mla" else 0
    bias_refs = refs[3:3 + nbias]
    o_ref = refs[3 + nbias]
    sa_ref, sb_ref, pa_ref, pb_ref = refs[4 + nbias:]
    hp = pl.program_id(0)
    i = pl.program_id(1)
    shared_kv = kind == "swa"
    chains = [(qi, hh) for qi in range(nq) for hh in range(hps)]
    nchain = len(chains)
    qs = [q_ref[qi, hh * dk:(hh + 1) * dk, :] for qi, hh in chains]
    ones = jnp.ones((ONES_ROWS, TB), BF16)

    def scores(c, kt):
        start = pl.multiple_of(kt * TB, TB)
        hh = chains[c][1]
        if shared_kv:
            return _scores_t(k_ref[0, pl.ds(start, TB), :], qs[c])
        return _scores_t(k_ref[pl.ds(start, TB), hh * dk:(hh + 1) * dk], qs[c])

    def vt_tile(c, kt):
        lo = 0 if shared_kv else chains[c][1] * dv
        return jnp.concatenate([vt_ref[kt, lo:lo + dv, :], ones], axis=0)

    states = tuple((m, jnp.ones((1, TB), F32), acc) for m, acc in (_softmax_init(dv) for _ in chains))

    def stage(score_next, kt_prev, s_cur, s_nxt, p_cur, p_prev, states, cs=range(nchain)):
        if score_next is not None:
            for c in cs:
                s_nxt[c] = score_next(c)
        new = list(states)
        for c in cs:
            m, alpha, acc = states[c]
            if p_prev is not None:
                acc = _pv_part(vt_tile(c, kt_prev(c)), p_prev[c], alpha, acc)
            m, alpha, p = _softmax_part(s_cur[c], m)
            p_cur[c] = p
            new[c] = (m, alpha, acc)
        return tuple(new)

    def finish(qi, last_tile, last_p):
        outs = []
        for hh in range(hps):
            c = qi * hps + hh
            m, alpha, acc = states[c]
            acc = _pv_part(vt_tile(c, last_tile(c)), last_p[c], alpha, acc)
            sink = sink_ref[hps * hp + hh] if kind == "swa" else None
            outs.append(_softmax_finish((m, acc), dv, sink))
        o_ref[qi * TB:(qi + 1) * TB, :] = jnp.concatenate(outs, axis=1).astype(o_ref.dtype)

    if kind == "mla":
        def latent(kt):
            return lambda c: scores(c, jnp.minimum(kt, ntl))

        def first_scores(cs):
            for c in cs:
                sa_ref[c] = scores(c, 0)

        groups = [range(qi * hps, (qi + 1) * hps) for qi in range(nq)]
        first_scores(groups[0])
        for qi, cs in enumerate(groups):
            def body(j, states, cs=cs):
                states = stage(latent(2 * j + 2), lambda c: 2 * j, sb_ref, sa_ref, pb_ref, pa_ref, states, cs)
                return stage(latent(2 * j + 3), lambda c: 2 * j + 1, sa_ref, sb_ref, pa_ref, pb_ref, states, cs)

            states = stage(latent(1), None, sa_ref, sb_ref, pa_ref, None, states, cs)
            states = lax.fori_loop(0, ntl // 2, body, states, unroll=16 if ntl % 32 == 0 else 1)
            if qi + 1 < nq:
                first_scores(groups[qi + 1])
            finish(qi, lambda c: ntl, pa_ref)
    else:
        tiles = []
        for qi in range(nq):
            ts = jnp.clip(i * nq + qi - 1, 0, ntl - 3)
            tiles.append([ts, ts + 1, ts + 2, ntl])

        def masked(s):
            def fn(c):
                qi, hh = chains[c]
                st = scores(c, tiles[qi][s])
                if s == 3:
                    return st
                return st + bias_refs[qi][0, hh if kind == "na" else 0, s * TB:(s + 1) * TB, :]
            return fn

        def tile_of(s):
            return lambda c: tiles[chains[c][0]][s]

        for c in range(nchain):
            sa_ref[c] = masked(0)(c)
        states = stage(masked(1), None, sa_ref, sb_ref, pa_ref, None, states)
        states = stage(masked(2), tile_of(0), sb_ref, sa_ref, pb_ref, pa_ref, states)
        states = stage(masked(3), tile_of(1), sa_ref, sb_ref, pa_ref, pb_ref, states)
        states = stage(None, tile_of(2), sb_ref, sa_ref, pb_ref, pa_ref, states)
        for qi in range(nq):
            finish(qi, tile_of(3), pb_ref)


def _attn_ctx_kernel(*refs, dk, dv, hps, has_sink, shared_kv):
    if has_sink:
        sink_ref, q_ref, k_ref, vt_ref, o_ref = refs
    else:
        q_ref, k_ref, vt_ref, o_ref = refs
    hp = pl.program_id(0)
    ones = jnp.ones((ONES_ROWS, TB), BF16)
    outs = []
    for hh in range(hps):
        k = k_ref[0] if shared_kv else k_ref[:, hh * dk:(hh + 1) * dk]
        lo = 0 if shared_kv else hh * dv
        st = _scores_t(k, q_ref[0, hh * dk:(hh + 1) * dk, :])
        vt = jnp.concatenate([vt_ref[0, lo:lo + dv, :], ones], axis=0)
        carry = _softmax_step(st, vt, _softmax_init(dv))
        outs.append(_softmax_finish(carry, dv, sink_ref[hps * hp + hh] if has_sink else None))
    o_ref[...] = jnp.concatenate(outs, axis=1).astype(o_ref.dtype)


def _attn(kind, qt, k, vt, *, dk, dv, ntl, bias=None, sink=None):
    nt, _, _ = qt.shape
    t = nt * TB
    nh = qt.shape[1] // dk
    shared_kv = kind == "swa"
    hps = 2 if kind == "mla" else 4
    kvs = 1 if shared_kv else hps
    nq = 2
    assert not shared_kv or nh // k.shape[0] == hps
    assert ntl % 2 == 0
    kern = functools.partial(_attn_kernel, kind=kind, dk=dk, dv=dv, ntl=ntl, hps=hps, nq=nq)
    local = kind != "mla"
    resident = dict(pipeline_mode=pl.Buffered(1)) if local else {}
    in_specs = [
        pl.BlockSpec((nq, hps * dk, TB), lambda hp, i, *_: (i, hp, 0)),
        pl.BlockSpec((1, t, dk), lambda hp, i, *_: (hp, 0, 0), **resident) if shared_kv
        else pl.BlockSpec((t, hps * dk), lambda hp, i, *_: (0, hp), **resident),
        pl.BlockSpec((nt, kvs * dv, TB), lambda hp, i, *_: (0, hp, 0), **resident),
    ]
    args = [qt, k, vt]
    nsp = 0
    if local:
        per_head = kind == "na"

        def bias_spec(qi):
            def bias_map(hp, i, *_):
                tile = i * nq + qi
                pat = jnp.where(tile == 0, 0, jnp.where(tile >= ntl - 1, 2, 1))
                return (pat, hp if per_head else 0, 0, 0)
            return pl.BlockSpec((1, hps if per_head else 1, 3 * TB, TB), bias_map)
        in_specs += [bias_spec(qi) for qi in range(nq)]
        args += [bias] * nq
    if kind == "swa":
        nsp = 1
        args = [sink] + args
    nchain = nq * hps
    o = pl.pallas_call(
        kern,
        grid_spec=pltpu.PrefetchScalarGridSpec(
            num_scalar_prefetch=nsp, grid=(nh // hps, ntl // nq), in_specs=in_specs,
            out_specs=pl.BlockSpec((nq * TB, hps * dv), lambda hp, i, *_: (i, hp)),
            scratch_shapes=[pltpu.VMEM((nchain, TB, TB), F32)] * 2 + [pltpu.VMEM((nchain, TB, TB), BF16)] * 2),
        out_shape=jax.ShapeDtypeStruct((ntl * TB, nh * dv), BF16),
        compiler_params=_cparams("arbitrary", "arbitrary"), name="attn_" + kind,
    )(*args)
    has_sink = kind == "swa"
    ckern = functools.partial(_attn_ctx_kernel, dk=dk, dv=dv, hps=hps, has_sink=has_sink,
                              shared_kv=shared_kv)
    cargs = ([sink] if has_sink else []) + [qt, k, vt]
    o_ctx = pl.pallas_call(
        ckern,
        grid_spec=pltpu.PrefetchScalarGridSpec(
            num_scalar_prefetch=nsp, grid=(nh // hps,),
            in_specs=[pl.BlockSpec((1, hps * dk, TB), lambda hp, *_: (ntl, hp, 0)),
                      pl.BlockSpec((1, TB, dk), lambda hp, *_: (hp, ntl, 0)) if shared_kv
                      else pl.BlockSpec((TB, hps * dk), lambda hp, *_: (ntl, hp)),
                      pl.BlockSpec((1, kvs * dv, TB), lambda hp, *_: (ntl, hp, 0))],
            out_specs=pl.BlockSpec((TB, hps * dv), lambda hp, *_: (0, hp))),
        out_shape=jax.ShapeDtypeStruct((TB, nh * dv), BF16),
        compiler_params=_cparams("arbitrary"), name="attn_ctx_" + kind,
    )(*cargs)
    return o, o_ctx


def _router_kernel(x_ref, g_ref, sh_ref, sc_ref, rt_ref, h_ref, aff_ref):
    h = _normmod(x_ref[...], g_ref[...], sh_ref[0, 0], sc_ref[0, 0])
    h_ref[...] = h.astype(BF16)
    lg = lax.dot_general(rt_ref[...], h, (((1,), (1,)), ((), ())),
                         precision=HIGHEST, preferred_element_type=F32)
    lg = lg - jnp.max(lg, axis=0, keepdims=True)
    e = jnp.exp(lg)
    aff_ref[...] = e / jnp.sum(e, axis=0, keepdims=True)


def _router(xa, g, mods, router_t, ntl):
    t, d = xa.shape
    ne = router_t.shape[0]

    def modspec(k):
        return pl.BlockSpec((1, 1, 1, d), lambda i: (k, i // ntl, 0, 0))

    return pl.pallas_call(
        _router_kernel, grid=(t // TB,),
        in_specs=[pl.BlockSpec((TB, d), lambda i: (i, 0)),
                  pl.BlockSpec((1, d), lambda i: (0, 0)),
                  modspec(3), modspec(4),
                  pl.BlockSpec((ne, d), lambda i: (0, 0))],
        out_specs=[pl.BlockSpec((TB, d), lambda i: (i, 0)),
                   pl.BlockSpec((ne, TB), lambda i: (0, i))],
        out_shape=[jax.ShapeDtypeStruct((t, d), BF16), jax.ShapeDtypeStruct((ne, t), F32)],
        compiler_params=_cparams("arbitrary"), name="router",
    )(xa, g.reshape(1, d), mods, mods, router_t)


def _topk_kernel(aff_ref, pos_ref, gate_ref, off_ref, *, cap, base):
    x = aff_ref[...]
    ne, nc, _ = x.shape
    bits = lax.bitcast_convert_type(x, jnp.int32)

    def count(mask):
        c = jnp.sum(jnp.where(mask, 1.0, 0.0), axis=1, keepdims=True)
        return jnp.sum(c, axis=2, keepdims=True)

    thr = jnp.zeros((ne, 1, 1), jnp.int32)
    for b in range(30, -1, -1):
        cand = thr | (1 << b)
        thr = jnp.where(count(bits >= cand) >= cap, cand, thr)

    ia = lax.broadcasted_iota(jnp.int32, (LANES, LANES), 0)
    ib = lax.broadcasted_iota(jnp.int32, (LANES, LANES), 1)
    upper = jnp.where(ia <= ib, 1.0, 0.0)
    ones = jnp.ones((LANES, LANES), F32)
    ca = lax.broadcasted_iota(jnp.int32, (nc, nc), 0)
    cb = lax.broadcasted_iota(jnp.int32, (nc, nc), 1)
    lower = jnp.where(cb < ca, 1.0, 0.0)

    def prefix(mask):
        m2 = jnp.where(mask, 1.0, 0.0).reshape(ne * nc, LANES)
        within = jnp.dot(m2, upper, precision=HIGHEST, preferred_element_type=F32)
        tot = jnp.dot(m2, ones, precision=HIGHEST, preferred_element_type=F32).reshape(ne, nc, LANES)
        offs = [jnp.dot(lower, tot[e], precision=HIGHEST, preferred_element_type=F32)[None]
                for e in range(ne)]
        off = jnp.concatenate(offs, axis=0)
        return within.reshape(ne, nc, LANES) + off, off

    gt = bits > thr
    eq = bits == thr
    need = cap - count(gt)
    eq_rank, _ = prefix(eq)
    sel = gt | (eq & (eq_rank <= need))
    sel_rank, off = prefix(sel)
    pos_ref[...] = jnp.where(sel, sel_rank - 1.0 + base, -1.0)
    gate_ref[...] = jnp.where(sel, x, 0.0)
    off_ref[...] = off.astype(jnp.int32) + base


def _topk(aff3, cap, base):
    ne, nc, _ = aff3.shape
    kern = functools.partial(_topk_kernel, cap=cap, base=base)
    spec = pl.BlockSpec((ne, nc, LANES), lambda i: (0, 0, 0))
    return pl.pallas_call(
        kern, grid=(1,), in_specs=[spec], out_specs=[spec, spec, spec],
        out_shape=[jax.ShapeDtypeStruct(aff3.shape, F32), jax.ShapeDtypeStruct(aff3.shape, F32),
                   jax.ShapeDtypeStruct(aff3.shape, jnp.int32)],
        compiler_params=_cparams("arbitrary"), name="topk",
    )(aff3)


def _ffn_kernel(offs_ref, pos_ref, gate_ref, h_ref, wg_ref, wu_ref, wd_ref, y_ref,
                xs_ref, xb_ref, gs_ref, *, ng, gt, nf, nrows):
    e = pl.program_id(0)
    j = pl.program_id(1)

    @pl.when(j == 0)
    def _():
        xs_ref[...] = jnp.zeros_like(xs_ref)
        gs_ref[...] = jnp.zeros_like(gs_ref)

    @pl.when(j < ng)
    def _():
        for s in range(gt):
            off = offs_ref[e, j * gt + s]
            w0 = pl.multiple_of((off // ROWS_F32) * ROWS_F32, ROWS_F32)
            prow = pos_ref[0, :, s * TB:(s + 1) * TB]
            grow = gate_ref[0, :, s * TB:(s + 1) * TB]
            ids = (w0 + lax.broadcasted_iota(jnp.int32, (GATHER_WIN, 1), 0)).astype(F32)
            hit = prow == ids
            onehot = jnp.where(hit, 1.0, 0.0).astype(BF16)
            xs_ref[pl.ds(w0, GATHER_WIN), :] += jnp.dot(
                onehot, h_ref[s * TB:(s + 1) * TB, :], preferred_element_type=F32)
            gsel = jnp.sum(jnp.where(hit, grow, 0.0), axis=1, keepdims=True)
            gs_ref[pl.ds(w0, GATHER_WIN), :] += jnp.broadcast_to(gsel, (GATHER_WIN, LANES))

    @pl.when(j == ng)
    def _():
        xb_ref[...] = xs_ref[0:nrows, :].astype(BF16)
        xs_ref[...] = jnp.zeros_like(xs_ref)

    @pl.when(j >= ng)
    def _():
        wg = wg_ref[0, 0].astype(BF16)
        wu = wu_ref[0, 0].astype(BF16)
        wd = wd_ref[0, 0].astype(BF16)
        for b in range(nrows // FFN_ROWS):
            rows = slice(b * FFN_ROWS, (b + 1) * FFN_ROWS)
            x = xb_ref[rows, :]
            a = jnp.dot(x, wg, preferred_element_type=F32)
            u = jnp.dot(x, wu, preferred_element_type=F32)
            hmid = (a * jax.nn.sigmoid(a) * u).astype(BF16)
            xs_ref[rows, :] += jnp.dot(hmid, wd, preferred_element_type=F32)

    @pl.when(j == ng + nf - 1)
    def _():
        y_ref[0, 0:nrows, :] = (xs_ref[0:nrows, :] * gs_ref[0:nrows, 0:1]).astype(y_ref.dtype)
        if y_ref.shape[1] > nrows:
            y_ref[0, nrows:, :] = jnp.zeros((y_ref.shape[1] - nrows, y_ref.shape[2]), y_ref.dtype)


def _ffn(offs, pos, gate, h, wg, wu, wd, *, layer, nslots):
    t, d = h.shape
    nt = t // TB
    _, ne, _, f = wg.shape
    gt = max(g for g in range(1, 17) if nt % g == 0)
    ng = nt // gt
    nf = f // FFN_FCHUNK
    nrows = -(-nslots // FFN_ROWS) * FFN_ROWS
    srows = -(-(nslots + TB) // ROWS_BF16) * ROWS_BF16
    srows = max(srows, nrows)
    xs_rows = max(nrows, nslots + GATHER_WIN)
    kern = functools.partial(_ffn_kernel, ng=ng, gt=gt, nf=nf, nrows=nrows)

    def fchunk(j):
        return jnp.clip(j - ng, 0, nf - 1)

    grid_spec = pltpu.PrefetchScalarGridSpec(
        num_scalar_prefetch=1, grid=(ne, ng + nf),
        in_specs=[
            pl.BlockSpec((1, 1, gt * TB), lambda e, j, o: (e, 0, jnp.minimum(j, ng - 1))),
            pl.BlockSpec((1, 1, gt * TB), lambda e, j, o: (e, 0, jnp.minimum(j, ng - 1))),
            pl.BlockSpec((gt * TB, d), lambda e, j, o: (jnp.minimum(j, ng - 1), 0)),
            pl.BlockSpec((1, 1, d, FFN_FCHUNK), lambda e, j, o: (layer, e, 0, fchunk(j))),
            pl.BlockSpec((1, 1, d, FFN_FCHUNK), lambda e, j, o: (layer, e, 0, fchunk(j))),
            pl.BlockSpec((1, 1, FFN_FCHUNK, d), lambda e, j, o: (layer, e, fchunk(j), 0)),
        ],
        out_specs=pl.BlockSpec((1, srows, d), lambda e, j, o: (e, 0, 0)),
        scratch_shapes=[pltpu.VMEM((xs_rows, d), F32), pltpu.VMEM((nrows, d), BF16),
                        pltpu.VMEM((xs_rows, LANES), F32)])
    return pl.pallas_call(
        kern, grid_spec=grid_spec,
        out_shape=jax.ShapeDtypeStruct((ne, srows, d), BF16),
        compiler_params=_cparams("arbitrary", "arbitrary"), name="ffn",
    )(offs, pos.reshape(ne, 1, t), gate.reshape(ne, 1, t), h, wg, wu, wd)


def _combine_kernel(*refs, ne, srows, final):
    offs_ref, cnt_ref, posc_ref, xa_ref, gate_ref, ys_ref = refs[:6]
    fg_ref = refs[6] if final else None
    o_ref, win_buf, spill_buf, win_sem, spill_sem = refs[6 + (1 if final else 0):]
    j = pl.program_id(0)
    slot = j % 2

    def window_copy(e, tile, buf_slot):
        row0 = pl.multiple_of(e * srows + (offs_ref[e, tile] // ROWS_BF16) * ROWS_BF16, ROWS_BF16)
        return pltpu.make_async_copy(
            ys_ref.at[pl.ds(row0, COMBINE_WIN), :],
            win_buf.at[buf_slot, pl.ds(e * COMBINE_WIN, COMBINE_WIN), :], win_sem.at[buf_slot])

    @pl.when(j == 0)
    def _():
        for e in range(ne):
            window_copy(e, 0, 0).start()

    @pl.when(j + 1 < pl.num_programs(0))
    def _():
        for e in range(ne):
            window_copy(e, j + 1, 1 - slot).start()

    for e in range(ne):
        window_copy(e, j, slot).wait()

    lane = lax.broadcasted_iota(jnp.int32, (1, LANES), 1)
    first = lane < COMBINE_WIN
    blocks = []
    for e in range(0, ne, 2):
        w0a = (offs_ref[e, j] // ROWS_BF16) * ROWS_BF16
        w0b = (offs_ref[e + 1, j] // ROWS_BF16) * ROWS_BF16
        ids = jnp.where(first, w0a + lane, w0b + lane - COMBINE_WIN).astype(F32)
        pcol = jnp.where(first, posc_ref[:, e:e + 1], posc_ref[:, e + 1:e + 2])
        blocks.append(jnp.where(pcol == ids, 1.0, 0.0).astype(BF16))
    onehot = jnp.concatenate(blocks, axis=1)
    acc = jnp.dot(onehot, win_buf[slot], preferred_element_type=F32)
    o_ref[...] = xa_ref[...] + gate_ref[0, 0] * acc

    wide = lax.broadcasted_iota(jnp.int32, (1, TB), 1)
    for e in range(ne):
        off = offs_ref[e, j]
        w0 = (off // ROWS_BF16) * ROWS_BF16

        @pl.when(off - w0 + cnt_ref[e, j] > COMBINE_WIN)
        def _(e=e, w0=w0):
            w1 = jnp.minimum(w0 + COMBINE_WIN, srows - TB)
            spill = pltpu.make_async_copy(
                ys_ref.at[pl.ds(pl.multiple_of(e * srows + w1, ROWS_BF16), TB), :], spill_buf, spill_sem)
            spill.start()
            spill.wait()
            ids = w1 + wide
            hit = (posc_ref[:, e:e + 1] == ids.astype(F32)) & (ids >= w0 + COMBINE_WIN)
            oh = jnp.where(hit, 1.0, 0.0).astype(BF16)
            o_ref[...] += gate_ref[0, 0] * jnp.dot(oh, spill_buf[...], preferred_element_type=F32)

    if final:
        x = o_ref[...]
        o_ref[...] = x * lax.rsqrt(jnp.mean(x * x, axis=-1, keepdims=True) + RMS_EPS) * fg_ref[...]


def _combine(offs, cnts, pos_cols, xa, mods, ys, ntl, final_g=None):
    t, d = xa.shape
    nt = ntl if final_g is not None else t // TB
    ne, srows, _ = ys.shape
    assert srows % ROWS_BF16 == 0
    ys2 = ys.reshape(ne * srows, d)

    assert ne % 2 == 0 and 2 * COMBINE_WIN == LANES
    in_specs = [
        pl.BlockSpec((TB, ne), lambda j, o, c: (j, 0)),
        pl.BlockSpec((TB, d), lambda j, o, c: (j, 0)),
        pl.BlockSpec((1, 1, 1, d), lambda j, o, c: (5, j // ntl, 0, 0)),
        pl.BlockSpec(memory_space=pl.ANY),
    ]
    args = [offs, cnts, pos_cols, xa, mods, ys2]
    if final_g is not None:
        in_specs.append(pl.BlockSpec((1, d), lambda j, o, c: (0, 0)))
        args.append(final_g.reshape(1, d))
    grid_spec = pltpu.PrefetchScalarGridSpec(
        num_scalar_prefetch=2, grid=(nt,), in_specs=in_specs,
        out_specs=pl.BlockSpec((TB, d), lambda j, o, c: (j, 0)),
        scratch_shapes=[pltpu.VMEM((2, ne * COMBINE_WIN, d), BF16), pltpu.VMEM((TB, d), BF16),
                        pltpu.SemaphoreType.DMA((2,)), pltpu.SemaphoreType.DMA(())])
    kern = functools.partial(_combine_kernel, ne=ne, srows=srows, final=final_g is not None)
    return pl.pallas_call(
        kern, grid_spec=grid_spec, out_shape=jax.ShapeDtypeStruct((nt * TB, d), F32),
        compiler_params=_cparams("arbitrary"), name="combine",
    )(*args)


def _rope_tables(seq, t, rot_dim, lane_layout):
    quarter = rot_dim // 4
    rows = seq // GRID_W
    inv_freq = ROPE_THETA ** (-jnp.arange(quarter, dtype=F32) / quarter)
    ang_r = jnp.arange(rows, dtype=F32)[:, None] * inv_freq
    ang_c = jnp.arange(GRID_W, dtype=F32)[:, None] * inv_freq

    def table(fn, fill):
        r = jnp.broadcast_to(fn(ang_r)[:, None, :], (rows, GRID_W, quarter))
        c = jnp.broadcast_to(fn(ang_c)[None, :, :], (rows, GRID_W, quarter))
        blk = jnp.concatenate([r, r, c, c], axis=2)
        parts, prev = [], 0
        for lo, hi in lane_layout:
            if lo > prev:
                parts.append(jnp.full((rows, GRID_W, lo - prev), fill, F32))
            parts.append(blk)
            prev = hi
        if prev < LANES:
            parts.append(jnp.full((rows, GRID_W, LANES - prev), fill, F32))
        lat = jnp.concatenate(parts, axis=2).reshape(seq, LANES)
        return jnp.concatenate([lat, jnp.full((t - seq, LANES), fill, F32)], axis=0)

    return table(jnp.cos, 1.0), table(jnp.sin, 0.0)


def _rope_partner_perm(rot_dim):
    q = rot_dim // 4
    src = np.concatenate([np.arange(q, 2 * q), np.arange(0, q), np.arange(3 * q, 4 * q), np.arange(2 * q, 3 * q)])
    sign = np.concatenate([-np.ones(q), np.ones(q), -np.ones(q), np.ones(q)]).astype(np.float32)
    return src, sign


def _swa_mask(ntl):
    out = []
    for i in (0, 1, ntl - 1):
        ts = min(max(i - 1, 0), ntl - 3)
        kpos = ts * TB + np.arange(3 * TB)[:, None]
        qpos = i * TB + np.arange(TB)[None, :]
        out.append(np.where(np.abs(kpos - qpos) <= SWA_WINDOW, 0.0, NEG).astype(np.float32))
    return jnp.asarray(np.stack(out)[:, None])


def _na_bias(rpb, rows, ntl):
    nh = rpb.shape[0]
    rpt = TB // GRID_W
    kr = min(NA_WIN_ROWS, rows)
    c = np.arange(GRID_W)
    cs = np.clip(c - NA_WIN_COLS // 2, 0, GRID_W - NA_WIN_COLS)
    okc = (c[:, None] >= cs[None, :]) & (c[:, None] < cs[None, :] + NA_WIN_COLS)
    dc = np.clip(c[:, None] - c[None, :] + NA_WIN_COLS - 1, 0, 2 * NA_WIN_COLS - 2)
    toep = jnp.where(okc[None, None], rpb[:, :, dc] * LOG2E, NEG)
    neg = jnp.full((nh, GRID_W, GRID_W), NEG, F32)
    out = []
    for i in (0, 1, ntl - 1):
        ts = min(max(i - 1, 0), ntl - 3)
        key_rows = []
        for krow in range(3 * rpt):
            r2 = ts * rpt + krow
            blocks = []
            for qrow in range(rpt):
                r = i * rpt + qrow
                rs = min(max(r - kr // 2, 0), rows - kr)
                blocks.append(toep[:, r2 - r + NA_WIN_ROWS - 1] if rs <= r2 < rs + kr else neg)
            key_rows.append(jnp.concatenate(blocks, axis=2))
        out.append(jnp.concatenate(key_rows, axis=1))
    return jnp.stack(out, axis=0).astype(F32)


def _mixer_na(xa, g, mods, w_qkv, w_o, rpb, ntl, seq):
    d = xa.shape[1]
    hd = d // NA_HEADS
    n = NA_HEADS * hd
    scale = hd ** -0.5 * LOG2E
    colscale = jnp.concatenate([jnp.full((n,), scale, F32), jnp.ones((2 * n,), F32)])
    qt, k, vt = _proj(xa, w_qkv.astype(BF16), ntl=ntl, mods=mods, norm_g=g, mod_idx=(0, 1),
                      colscale=colscale, splits=[(0, n), (n, 2 * n), (2 * n, 3 * n)],
                      out_t=[True, False, True], name="na_qkv")
    bias = _na_bias(rpb, seq // GRID_W, ntl)
    o, o_ctx = _attn("na", qt, k, vt, dk=hd, dv=hd, ntl=ntl, bias=bias)
    (xa,) = _proj(o, w_o.astype(BF16), x_tail=o_ctx, ntl=ntl, mods=mods, resid=xa, gate_idx=2,
                  out_dtypes=[F32], name="na_out")
    return xa


def _mixer_swa(xa, g, mods, w_qkv, w_o, sink, ntl, seq):
    t, d = xa.shape
    hd = d // SWA_Q_HEADS
    nq, nkv = SWA_Q_HEADS * hd, SWA_KV_HEADS * hd
    scale = hd ** -0.5 * LOG2E
    nrope = nq + nkv
    src, sign = _rope_partner_perm(hd)
    nblk = nrope // hd
    src_full = (np.arange(nblk)[:, None] * hd + src[None, :]).reshape(-1)
    sign_full = np.tile(sign, nblk)
    w2 = w_qkv[:, src_full] * sign_full[None, :]
    cos, sin = _rope_tables(seq, t, hd, [(0, hd), (hd, 2 * hd)])
    colscale = jnp.concatenate([jnp.full((nq,), scale, F32), jnp.ones((2 * nkv,), F32)])
    qt, k, vt = _proj(xa, w_qkv.astype(BF16), ntl=ntl, mods=mods, norm_g=g, mod_idx=(0, 1),
                      w2=w2.astype(BF16), cos=cos, sin=sin, colscale=colscale,
                      splits=[(0, nq), (nq, nrope), (nrope, nrope + nkv)],
                      out_t=[True, False, True], name="swa_qkv")
    k = jnp.transpose(k.reshape(t, SWA_KV_HEADS, hd), (1, 0, 2))
    o, o_ctx = _attn("swa", qt, k, vt, dk=hd, dv=hd, ntl=ntl, bias=_swa_mask(ntl),
                     sink=sink.astype(F32) * LOG2E)
    (xa,) = _proj(o, w_o.astype(BF16), x_tail=o_ctx, ntl=ntl, mods=mods, resid=xa, gate_idx=2,
                  out_dtypes=[F32], name="swa_out")
    return xa


def _mixer_mla(xa, g, mods, w_dq, q_norm, w_uq, w_dkv, kv_norm, w_ukv, w_o, ntl, seq):
    t, d = xa.shape
    nh = MLA_HEADS
    qr = w_dq.shape[1]
    kvr = kv_norm.shape[0]
    nope, rope, dv = MLA_NOPE_DIM, MLA_ROPE_DIM, MLA_V_DIM
    qk = nope + rope
    pad = LANES - qk
    scale = qk ** -0.5 * LOG2E
    kin = kvr + LANES
    w_dn = jnp.concatenate([w_dq, w_dkv, jnp.zeros((d, LANES - rope), F32)], axis=1)
    g_dn = jnp.concatenate([q_norm, kv_norm, jnp.ones((LANES,), F32)])
    cqn, kvin = _proj(xa, w_dn.astype(BF16), ntl=ntl, mods=mods, norm_g=g, mod_idx=(0, 1),
                      rms_segs=[(0, qr), (qr, qr + kvr)], rms_g=g_dn,
                      splits=[(0, qr), (qr, qr + kin)], name="mla_down")
    src, sign = _rope_partner_perm(rope)
    src_head = np.concatenate([np.arange(nope), nope + src, np.arange(qk, LANES)])
    sign_head = np.concatenate([np.zeros(nope, np.float32), sign, np.zeros(pad, np.float32)])
    src_full = (np.arange(nh)[:, None] * LANES + src_head[None, :]).reshape(-1)
    sign_full = np.tile(sign_head, nh)
    cos, sin = _rope_tables(seq, t, rope, [(nope, qk)])
    wq = jnp.pad(w_uq.reshape(qr, nh, qk), ((0, 0), (0, 0), (0, pad))).reshape(qr, nh * LANES)
    wq2 = wq[:, src_full] * sign_full[None, :]
    (qt,) = _proj(cqn, wq.astype(BF16), ntl=ntl, w2=wq2.astype(BF16), cos=cos, sin=sin,
                  colscale=jnp.full((nh * LANES,), scale, F32), out_t=[True], name="mla_q")
    w_ukv3 = w_ukv.reshape(kvr, nh, nope + dv)
    wk_top = jnp.pad(w_ukv3[:, :, :nope], ((0, 0), (0, 0), (0, LANES - nope)))
    eye = jnp.eye(rope, dtype=F32)
    wk_rope = jnp.pad(eye, ((0, LANES - rope), (nope, pad)))
    wk_bot = jnp.broadcast_to(wk_rope[:, None, :], (LANES, nh, LANES))
    wk = jnp.concatenate([wk_top, wk_bot], axis=0).reshape(kin, nh * LANES)
    wk2 = wk[:, src_full] * sign_full[None, :]
    wv = jnp.concatenate([w_ukv3[:, :, nope:].reshape(kvr, nh * dv), jnp.zeros((LANES, nh * dv), F32)], axis=0)
    nk = nh * LANES
    k, vt = _proj(kvin, jnp.concatenate([wk, wv], axis=1).astype(BF16), ntl=ntl, w2=wk2.astype(BF16),
                  cos=cos, sin=sin, splits=[(0, nk), (nk, nk + nh * dv)], out_t=[False, True], name="mla_kv")
    o, o_ctx = _attn("mla", qt, k, vt, dk=LANES, dv=dv, ntl=ntl)
    (xa,) = _proj(o, w_o.astype(BF16), x_tail=o_ctx, ntl=ntl, mods=mods, resid=xa, gate_idx=2,
                  out_dtypes=[F32], name="mla_out")
    return xa


def _moe(xa, g, mods, router, w_gate, w_up, w_down, layer, ntl, seq, with_ctx, final_g=None):
    t, d = xa.shape
    ne = router.shape[1]
    nctx = t - seq
    h, aff = _router(xa, g, mods, router.T, ntl)
    cap_l = EC_CAPACITY_FACTOR * seq // ne
    pos_l, gate_l, off_l = _topk(aff[:, :seq].reshape(ne, seq // CHUNK, CHUNK), cap_l, 0)
    per_tile = TB // CHUNK
    pos = [pos_l.reshape(ne, seq)]
    gate = [gate_l.reshape(ne, seq)]
    offs = [off_l[:, ::per_tile, 0]]
    if with_ctx:
        cap_c = EC_CAPACITY_FACTOR * nctx // ne
        cpad = 8 * CHUNK
        aff_c = jnp.concatenate([aff[:, seq:], jnp.full((ne, cpad - nctx), -1.0, F32)], axis=1)
        pos_c, gate_c, off_c = _topk(aff_c.reshape(ne, 8, CHUNK), cap_c, cap_l)
        pos.append(pos_c.reshape(ne, cpad)[:, :nctx])
        gate.append(gate_c.reshape(ne, cpad)[:, :nctx])
        offs.append(off_c[:, 0:nctx // CHUNK:per_tile, 0])
        nslots = cap_l + cap_c
    else:
        pos.append(jnp.full((ne, nctx), -1.0, F32))
        gate.append(jnp.zeros((ne, nctx), F32))
        offs.append(jnp.full((ne, nctx // TB), cap_l, jnp.int32))
        nslots = cap_l
    pos = jnp.concatenate(pos, axis=1)
    gate = jnp.concatenate(gate, axis=1)
    offs = jnp.concatenate(offs + [jnp.full((ne, 1), nslots, jnp.int32)], axis=1)
    cnts = offs[:, 1:] - offs[:, :-1]
    ys = _ffn(offs, pos, gate, h, w_gate, w_up, w_down, layer=layer, nslots=nslots)
    return _combine(offs, cnts, pos.T, xa, mods, ys, ntl, final_g=final_g)


def kernel(x, c, ctx, c_ctx, ada_w, ada_b, norm_mix, norm_ffn, na_w_qkv, na_w_o, na_rpb, swa_w_qkv, swa_w_o, swa_sink, mla_w_dq, mla_q_norm, mla_w_uq, mla_w_dkv, mla_kv_norm, mla_w_ukv, mla_w_o, moe_router, moe_w_gate, moe_w_up, moe_w_down, final_norm):
    assert x.shape[0] == 1 and c.shape[0] == 1 and ctx.shape[0] == 1
    seq, d = x.shape[1], x.shape[2]
    nctx = ctx.shape[1]
    assert seq % TB == 0 and nctx == TB and seq // TB >= 4
    depth = ada_w.shape[0]
    ntl = seq // TB
    xa = jnp.concatenate([x[0], ctx[0]], axis=0)
    cs = jnp.concatenate([c, c_ctx[None, :], jnp.zeros((6, d), F32)], axis=0)
    ada = _ada(cs, ada_w, ada_b)
    mods_all = jnp.transpose(ada[:, :2].reshape(depth, 2, 6, d), (0, 2, 1, 3))[:, :, :, None, :]
    for i in range(depth):
        mods = mods_all[i]
        kind, slot = i % N_MIXERS, i // N_MIXERS
        if kind == 0:
            xa = _mixer_na(xa, norm_mix[i], mods, na_w_qkv[slot], na_w_o[slot], na_rpb[slot], ntl, seq)
        elif kind == 1:
            xa = _mixer_swa(xa, norm_mix[i], mods, swa_w_qkv[slot], swa_w_o[slot], swa_sink[slot], ntl, seq)
        else:
            xa = _mixer_mla(xa, norm_mix[i], mods, mla_w_dq[slot], mla_q_norm[slot], mla_w_uq[slot],
                            mla_w_dkv[slot], mla_kv_norm[slot], mla_w_ukv[slot], mla_w_o[slot], ntl, seq)
        last = i == depth - 1
        xa = _moe(xa, norm_ffn[i], mods, moe_router[i], moe_w_gate, moe_w_up, moe_w_down, i,
                  ntl, seq, with_ctx=not last, final_g=final_norm if last else None)
    return xa[None]
```
